```python
import math
import jax, jax.numpy as jnp
from jax import lax
import numpy as np

D_MODEL = 1024
BATCH = 8
SEQ = 8192
DEPTH = 1
DEC_BATCH = 8
DEC_SEQ = 64
PAST_LEN = 2048

CHUNK = 64
LEFT_CHUNKS = 8
N_HEADS_A = 8
HEAD_DIM_A = 64
REL_CLIP = 128
N_HEADS_B = 8
DK_B = 64
DV_B = 64
CONV_W = 4
N_GROUPS = 4
EXPERTS_PER_GROUP = 8
N_EXPERTS = N_GROUPS * EXPERTS_PER_GROUP
TOP_K = 2
D_FF_EXPERT = 512
MOE_BLOCK = 256
RMS_EPS = 1e-6

WIDTH_A = N_HEADS_A * HEAD_DIM_A
WIDTH_B_QK = N_HEADS_B * DK_B
WIDTH_B_V = N_HEADS_B * DV_B
CONV_CH = 2 * WIDTH_B_QK + WIDTH_B_V
PROJ_SPLITS = (WIDTH_A, WIDTH_A, WIDTH_A, CONV_CH, WIDTH_B_V, N_HEADS_B, N_HEADS_B, D_MODEL, D_MODEL)
D_IN_PROJ = 3 * WIDTH_A + CONV_CH + WIDTH_B_V + 2 * N_HEADS_B + 2 * D_MODEL

kernel_name = 'hybrid_chunkband_gdn_hmoe_step'


def rmsnorm(x, g):
    xf = x.astype(jnp.float32)
    y = xf * lax.rsqrt(jnp.mean(xf * xf, axis=-1, keepdims=True) + RMS_EPS)
    return (y * g.astype(jnp.float32)).astype(x.dtype)


def l2norm(x):
    xf = x.astype(jnp.float32)
    return (xf * lax.rsqrt(jnp.sum(xf * xf, axis=-1, keepdims=True) + RMS_EPS)).astype(x.dtype)


def split_cols(a, sizes):
    out, off = [], 0
    for s in sizes:
        out.append(a[..., off:off + s])
        off += s
    return out


def rel_lookup(rel_bias, dist):
    return rel_bias[:, jnp.clip(dist, -REL_CLIP, REL_CLIP) + REL_CLIP]


def band_attend(q, k, v, bias, valid):
    s = jnp.einsum('bqhd,bkhd->bhqk', q, k).astype(jnp.float32) * (HEAD_DIM_A ** -0.5)
    s = s + bias.astype(jnp.float32)[None]
    s = jnp.where(valid[None, None, None, :], s, -1e30)
    p = jax.nn.softmax(s, axis=-1).astype(v.dtype)
    return jnp.einsum('bhqk,bkhd->bqhd', p, v)


def chunk_band_attention_prompt(q, k, v, rel_bias):
    B, T, H, Dh = q.shape
    left = LEFT_CHUNKS * CHUNK
    band = left + CHUNK
    nc = T // CHUNK
    pad = ((0, 0), (left, 0), (0, 0), (0, 0))
    kp, vp = jnp.pad(k, pad), jnp.pad(v, pad)
    i = jnp.arange(CHUNK)
    j = jnp.arange(band)
    bias = rel_lookup(rel_bias, i[:, None] - j[None, :] + left)

    def one_chunk(c):
        s0 = c * CHUNK
        qc = lax.dynamic_slice_in_dim(q, s0, CHUNK, axis=1)
        kc = lax.dynamic_slice_in_dim(kp, s0, band, axis=1)
        vc = lax.dynamic_slice_in_dim(vp, s0, band, axis=1)
        valid = (s0 - left + j) >= 0
        return band_attend(qc, kc, vc, bias, valid)

    o = lax.map(one_chunk, jnp.arange(nc))
    o = jnp.moveaxis(o, 0, 1).reshape(B, T, H, Dh)
    keep = min(left, T)
    return o, k[:, T - keep:], v[:, T - keep:]


def chunk_band_attention_sample(q, k, v, past_k, past_v, rel_bias):
    T = q.shape[1]
    lc = past_k.shape[1]
    kk = jnp.concatenate([past_k.astype(k.dtype), k], axis=1)
    vv = jnp.concatenate([past_v.astype(v.dtype), v], axis=1)
    i = jnp.arange(T)
    j = jnp.arange(lc + T)
    bias = rel_lookup(rel_bias, i[:, None] - j[None, :] + lc)
    valid = jnp.ones((lc + T,), dtype=bool)
    return band_attend(q, kk, vv, bias, valid), k, v


def causal_depthwise_conv(x, buf, w):
    T = x.shape[1]
    xx = jnp.concatenate([buf.astype(x.dtype), x], axis=1)
    y = xx[:, 0:T] * w[0]
    for t in range(1, CONV_W):
        y = y + xx[:, t:t + T] * w[t]
    return y, xx[:, xx.shape[1] - (CONV_W - 1):]


def gated_delta_chunked(q, k, v, g, beta, s0):
    f32 = jnp.float32
    q, k, v, g, beta, s0 = (t.astype(f32) for t in (q, k, v, g, beta, s0))
    B, T, H, DK = q.shape
    DV = v.shape[-1]
    C = min(CHUNK, T)
    nc = T // C

    def to_chunks(a):
        a = a.reshape((B, nc, C, H) + a.shape[3:])
        return jnp.moveaxis(a, 3, 1)

    q, k, v, g, beta = to_chunks(q), to_chunks(k), to_chunks(v), to_chunks(g), to_chunks(beta)
    gc = jnp.cumsum(g, axis=-1)
    idx = jnp.arange(C)
    incl = idx[:, None] >= idx[None, :]
    strict = idx[:, None] > idx[None, :]
    decay = jnp.exp(jnp.where(incl, gc[..., :, None] - gc[..., None, :], -jnp.inf))
    kb = k * beta[..., None]
    lower = jnp.einsum('bhnid,bhnjd->bhnij', kb, k) * jnp.where(strict, decay, 0.0)
    a = lower + jnp.eye(C, dtype=f32)
    rhs = jnp.concatenate([v * beta[..., None], kb * jnp.exp(gc)[..., None]], axis=-1)
    sol = lax.linalg.triangular_solve(a, rhs, left_side=True, lower=True, unit_diagonal=True)
    value, kcum = sol[..., :DV], sol[..., DV:]
    attn_qk = jnp.einsum('bhnid,bhnjd->bhnij', q, k) * decay
    q_dec = q * jnp.exp(gc)[..., None]
    k_dec = k * jnp.exp(gc[..., -1:] - gc)[..., None]
    g_tot = jnp.exp(gc[..., -1])
    xs = tuple(jnp.moveaxis(t, 2, 0) for t in (q_dec, attn_qk, value, kcum, k_dec, g_tot))

    def step(S, inp):
        qd, aqk, val, kc, kd, gt = inp
        u = val - jnp.einsum('bhck,bhkv->bhcv', kc, S)
        o = jnp.einsum('bhck,bhkv->bhcv', qd, S) + jnp.einsum('bhij,bhjv->bhiv', aqk, u)
        S = S * gt[..., None, None] + jnp.einsum('bhck,bhcv->bhkv', kd, u)
        return S, o

    s_new, o = lax.scan(step, s0, xs)
    o = jnp.transpose(o, (1, 0, 3, 2, 4)).reshape(B, T, H, DV)
    return o, s_new


def hier_moe(x, w_route_group, w_route_expert, w_gate, w_up, w_down):
    N, D = x.shape
    logits_g = (x @ w_route_group).astype(jnp.float32)
    p_g = jax.nn.softmax(logits_g, axis=-1)
    grp = jnp.argmax(logits_g, axis=-1)
    rows = jnp.arange(N)
    p_grp = p_g[rows, grp]
    logits_e = (x @ w_route_expert).astype(jnp.float32).reshape(N, N_GROUPS, EXPERTS_PER_GROUP)
    p_e = jax.nn.softmax(logits_e[rows, grp], axis=-1)
    top_p, top_i = lax.top_k(p_e, TOP_K)
    w = top_p / jnp.sum(top_p, axis=-1, keepdims=True) * p_grp[:, None]
    eid = grp[:, None] * EXPERTS_PER_GROUP + top_i

    nk = N * TOP_K
    e_flat = eid.reshape(-1)
    tok_flat = jnp.repeat(rows, TOP_K)
    w_flat = w.reshape(-1)
    order = jnp.argsort(e_flat)
    e_s, tok_s, w_s = e_flat[order], tok_flat[order], w_flat[order]
    counts = jnp.bincount(e_flat, length=N_EXPERTS)
    starts = jnp.cumsum(counts) - counts
    pcounts = (counts + MOE_BLOCK - 1) // MOE_BLOCK * MOE_BLOCK
    pends = jnp.cumsum(pcounts)
    pstarts = pends - pcounts
    dest = pstarts[e_s] + (jnp.arange(nk) - starts[e_s])
    nb = -(-nk // MOE_BLOCK) + N_EXPERTS
    npad = nb * MOE_BLOCK
    slot_tok = jnp.full((npad,), N, dtype=jnp.int32).at[dest].set(tok_s.astype(jnp.int32))
    slot_w = jnp.zeros((npad,), jnp.float32).at[dest].set(w_s)
    blk_start = jnp.arange(nb) * MOE_BLOCK
    blk_e = jnp.minimum(jnp.sum(pends[None, :] <= blk_start[:, None], axis=1), N_EXPERTS - 1)
    x_pad = jnp.concatenate([x, jnp.zeros((1, D), x.dtype)], axis=0)

    def run_block(args):
        tok_b, w_b, e_b = args
        xb = x_pad[tok_b]
        hb = jax.nn.silu(xb @ w_gate[e_b]) * (xb @ w_up[e_b])
        return (hb @ w_down[e_b]).astype(jnp.float32) * w_b[:, None]

    out = lax.map(run_block, (slot_tok.reshape(nb, MOE_BLOCK), slot_w.reshape(nb, MOE_BLOCK), blk_e))
    y = jax.ops.segment_sum(out.reshape(npad, D), slot_tok, num_segments=N + 1)[:N]
    return y.astype(x.dtype)


def trunk_layer(x, past_k, past_v, s0, conv_buf, lw, prompt):
    (norm_mix, w_in, rel_bias, conv_w, a_log, dt_bias, gdn_norm, w_branch_a, w_branch_b,
     w_out, norm_ffn, w_route_group, w_route_expert, w_gate, w_up, w_down) = lw
    B, T, D = x.shape
    h = rmsnorm(x, norm_mix)
    proj = h @ w_in
    qa, ka, va, xbc, gb, b_logit, a_logit, gate_a, gate_b = split_cols(proj, PROJ_SPLITS)
    qa = qa.reshape(B, T, N_HEADS_A, HEAD_DIM_A)
    ka = ka.reshape(B, T, N_HEADS_A, HEAD_DIM_A)
    va = va.reshape(B, T, N_HEADS_A, HEAD_DIM_A)
    if prompt:
        o_a, new_k, new_v = chunk_band_attention_prompt(qa, ka, va, rel_bias)
        conv_buf = jnp.zeros((B, CONV_W - 1, CONV_CH), x.dtype)
        s0 = jnp.zeros((B, N_HEADS_B, DK_B, DV_B), jnp.float32)
    else:
        o_a, new_k, new_v = chunk_band_attention_sample(qa, ka, va, past_k, past_v, rel_bias)
    xbc, new_conv = causal_depthwise_conv(xbc, conv_buf, conv_w)
    xbc = jax.nn.silu(xbc)
    qb, kb, vb = split_cols(xbc, (WIDTH_B_QK, WIDTH_B_QK, WIDTH_B_V))
    qb = l2norm(qb.reshape(B, T, N_HEADS_B, DK_B)) * (DK_B ** -0.5)
    kb = l2norm(kb.reshape(B, T, N_HEADS_B, DK_B))
    vb = vb.reshape(B, T, N_HEADS_B, DV_B)
    beta = jax.nn.sigmoid(b_logit.astype(jnp.float32))
    g = -jnp.exp(a_log.astype(jnp.float32)) * jax.nn.softplus(
        a_logit.astype(jnp.float32) + dt_bias.astype(jnp.float32))
    o_b, new_s = gated_delta_chunked(qb, kb, vb, g, beta, s0)
    o_b = rmsnorm(o_b.astype(x.dtype), gdn_norm) * jax.nn.silu(gb.reshape(B, T, N_HEADS_B, DV_B))
    a_out = o_a.reshape(B, T, WIDTH_A) @ w_branch_a
    b_out = o_b.reshape(B, T, WIDTH_B_V) @ w_branch_b
    merged = jax.nn.sigmoid(gate_a) * a_out + jax.nn.sigmoid(gate_b) * b_out
    x = x + merged @ w_out
    h = rmsnorm(x, norm_ffn)
    x = x + hier_moe(h.reshape(B * T, D), w_route_group, w_route_expert, w_gate, w_up, w_down).reshape(B, T, D)
    return x, (new_k, new_v, new_s, new_conv)


def setup_inputs(seed: int = 0) -> dict:
    key = jax.random.key(seed)
    ks = jax.random.split(key, 24)
    f32 = jnp.float32
    a_cache = min(LEFT_CHUNKS * CHUNK, PAST_LEN)

    def nrm(k, shape, scale):
        return jax.random.normal(k, shape, f32) * scale

    def gain(k, shape):
        return 1.0 + 0.02 * jax.random.normal(k, shape, f32)

    dt = jnp.exp(jax.random.uniform(ks[8], (DEPTH, N_HEADS_B), f32, math.log(1e-3), math.log(1e-1)))
    return {
        'x_prompt': nrm(ks[0], (BATCH, SEQ, D_MODEL), 1.0),
        'x_sample': nrm(ks[1], (DEC_BATCH, DEC_SEQ, D_MODEL), 1.0),
        'cache_attn_k': nrm(ks[2], (DEPTH, DEC_BATCH, a_cache, N_HEADS_A, HEAD_DIM_A), 1.0),
        'cache_attn_v': nrm(ks[3], (DEPTH, DEC_BATCH, a_cache, N_HEADS_A, HEAD_DIM_A), 1.0),
        'state_delta': nrm(ks[4], (DEPTH, DEC_BATCH, N_HEADS_B, DK_B, DV_B), 0.1),
        'state_conv': nrm(ks[5], (DEPTH, DEC_BATCH, CONV_W - 1, CONV_CH), 1.0),
        'norm_mix': gain(ks[6], (DEPTH, D_MODEL)),
        'w_in': nrm(ks[7], (DEPTH, D_MODEL, D_IN_PROJ), D_MODEL ** -0.5),
        'rel_bias': nrm(ks[9], (DEPTH, N_HEADS_A, 2 * REL_CLIP + 1), 0.1),
        'conv_w': nrm(ks[10], (DEPTH, CONV_W, CONV_CH), CONV_W ** -0.5),
        'a_log': jnp.log(jax.random.uniform(ks[11], (DEPTH, N_HEADS_B), f32, 1.0, 16.0)),
        'dt_bias': dt + jnp.log(-jnp.expm1(-dt)),
        'gdn_norm': gain(ks[12], (DEPTH, DV_B)),
        'w_branch_a': nrm(ks[13], (DEPTH, WIDTH_A, D_MODEL), WIDTH_A ** -0.5),
        'w_branch_b': nrm(ks[14], (DEPTH, WIDTH_B_V, D_MODEL), WIDTH_B_V ** -0.5),
        'w_out': nrm(ks[15], (DEPTH, D_MODEL, D_MODEL), D_MODEL ** -0.5),
        'norm_ffn': gain(ks[16], (DEPTH, D_MODEL)),
        'w_route_group': nrm(ks[17], (DEPTH, D_MODEL, N_GROUPS), D_MODEL ** -0.5),
        'w_route_expert': nrm(ks[18], (DEPTH, D_MODEL, N_EXPERTS), D_MODEL ** -0.5),
        'w_gate': nrm(ks[19], (DEPTH, N_EXPERTS, D_MODEL, D_FF_EXPERT), D_MODEL ** -0.5),
        'w_up': nrm(ks[20], (DEPTH, N_EXPERTS, D_MODEL, D_FF_EXPERT), D_MODEL ** -0.5),
        'w_down': nrm(ks[21], (DEPTH, N_EXPERTS, D_FF_EXPERT, D_MODEL), D_FF_EXPERT ** -0.5),
        'norm_final': gain(ks[22], (D_MODEL,)),
    }


def reference(x_prompt, x_sample, cache_attn_k, cache_attn_v, state_delta, state_conv,
              norm_mix, w_in, rel_bias, conv_w, a_log, dt_bias, gdn_norm, w_branch_a, w_branch_b,
              w_out, norm_ffn, w_route_group, w_route_expert, w_gate, w_up, w_down, norm_final):
    xp, xs = x_prompt, x_sample
    pk, pv, ps, pc = [], [], [], []
    sk, sv, ss, sc = [], [], [], []
    for l in range(DEPTH):
        lw = (norm_mix[l], w_in[l], rel_bias[l], conv_w[l], a_log[l], dt_bias[l], gdn_norm[l],
              w_branch_a[l], w_branch_b[l], w_out[l], norm_ffn[l], w_route_group[l],
              w_route_expert[l], w_gate[l], w_up[l], w_down[l])
        xp, (k_p, v_p, s_p, c_p) = trunk_layer(xp, None, None, None, None, lw, True)
        xs, (k_s, v_s, s_s, c_s) = trunk_layer(xs, cache_attn_k[l], cache_attn_v[l], state_delta[l],
                                               state_conv[l], lw, False)
        pk.append(k_p); pv.append(v_p); ps.append(s_p); pc.append(c_p)
        sk.append(k_s); sv.append(v_s); ss.append(s_s); sc.append(c_s)
    y_prompt = rmsnorm(xp, norm_final)
    y_sample = rmsnorm(xs, norm_final)
    return (y_prompt, y_sample,
            jnp.stack(pk), jnp.stack(pv), jnp.stack(ps), jnp.stack(pc),
            jnp.stack(sk), jnp.stack(sv), jnp.stack(ss), jnp.stack(sc))
```

```python
import functools

import jax
import jax.numpy as jnp
from jax import lax
from jax.experimental import pallas as pl
from jax.experimental.pallas import tpu as pltpu

F32 = jnp.float32
BF16 = jnp.bfloat16
I32 = jnp.int32

RMS_EPS = 1e-6
CHUNK = 64
LEFT_CHUNKS = 8
BAND = (LEFT_CHUNKS + 1) * CHUNK
N_HEADS = 8
HEAD_DIM = 64
N_PAIRS = N_HEADS // 2
WIDTH = N_HEADS * HEAD_DIM
REL_CLIP = 128
CONV_W = 4
N_GROUPS = 4
EXPERTS_PER_GROUP = 8
N_EXPERTS = N_GROUPS * EXPERTS_PER_GROUP
LANES = 128
CONV_TAIL = 8
MOE_ROWS = 256
VMEM_LIMIT = 56 * 1024 * 1024


def _dot(a, b):
    return jnp.dot(a, b, preferred_element_type=F32)


def _dot_nt(a, b):
    return lax.dot_general(a, b, (((1,), (1,)), ((), ())), preferred_element_type=F32)


def _dot_tn(a, b):
    return lax.dot_general(a, b, (((0,), (0,)), ((), ())), preferred_element_type=F32)


def _sigmoid(x):
    return 1.0 / (1.0 + jnp.exp(-x))


def _split2(x):
    hi = x.astype(BF16)
    lo = (x - hi.astype(F32)).astype(BF16)
    return hi, lo


def _split3(x):
    hi = x.astype(BF16)
    r = x - hi.astype(F32)
    mid = r.astype(BF16)
    lo = (r - mid.astype(F32)).astype(BF16)
    return hi, mid, lo


def _dot3_rhs01(x, mat):
    hi, mid, lo = _split3(x)
    return (_dot(hi, mat) + _dot(mid, mat)) + _dot(lo, mat)


def _dot3_lhs01(mat, x):
    hi, mid, lo = _split3(x)
    return (_dot(mat, hi) + _dot(mat, mid)) + _dot(mat, lo)


def _group_sum(x2, bd):
    hi, lo = _split2(x2)
    return _dot(hi, bd) + _dot(lo, bd)


def _params(sem):
    return pltpu.CompilerParams(dimension_semantics=sem, vmem_limit_bytes=VMEM_LIMIT)


def _inproj_kernel(x_ref, g_ref, wqkv_ref, wxbc_ref, wgb_ref, wab_ref, wgate_ref,
                   qkv_ref, kv_ref, xbc_ref, gb_ref, ab_ref, gate_ref):
    x = x_ref[...]
    h = x * lax.rsqrt(jnp.mean(x * x, axis=-1, keepdims=True) + RMS_EPS) * g_ref[...]
    hb = h.astype(BF16)
    qkv = _dot(hb, wqkv_ref[...])
    qkv_ref[...] = qkv.astype(BF16)
    kv_ref[...] = qkv[:, WIDTH:]
    xbc_ref[...] = _dot(hb, wxbc_ref[...])
    gb_ref[...] = _dot(hb, wgb_ref[...])
    ab_ref[...] = _dot(hb, wab_ref[...])
    gate_ref[...] = _dot(hb, wgate_ref[...])


def _inproj(x, g, w, tm):
    n, d = x.shape
    wqkv, wxbc, wgb, wab, wgate = w
    row = lambda i: (i, 0)
    const = lambda i: (0, 0)
    widths = (3 * WIDTH, 2 * WIDTH, 3 * WIDTH, WIDTH, LANES, 2 * d)
    dtypes = (BF16, F32, F32, F32, F32, F32)
    return pl.pallas_call(
        _inproj_kernel,
        grid=(n // tm,),
        in_specs=[pl.BlockSpec((tm, d), row), pl.BlockSpec((1, d), const)]
        + [pl.BlockSpec(m.shape, const) for m in w],
        out_specs=[pl.BlockSpec((tm, c), row) for c in widths],
        out_shape=[jax.ShapeDtypeStruct((n, c), t) for c, t in zip(widths, dtypes)],
        compiler_params=_params(("parallel",)),
        name="inproj",
    )(x, g, wqkv, wxbc, wgb, wab, wgate)


def _attn_kernel(q_ref, kp_ref, kc_ref, vp_ref, vc_ref, bias_ref, o_ref, kw_ref, vw_ref,
                 *, cpb, mask_first):
    left = LEFT_CHUNKS * CHUNK
    kw_ref[0:left, :] = kp_ref[...]
    kw_ref[left:, :] = kc_ref[...]
    vw_ref[0:left, :] = vp_ref[...]
    vw_ref[left:, :] = vc_ref[...]
    lane = lax.broadcasted_iota(I32, (CHUNK, LANES), 1)
    m0 = lane < HEAD_DIM
    first_block = pl.program_id(1) == 0

    def chunk(i, carry):
        r0 = pl.multiple_of(i * CHUNK, CHUNK)
        if mask_first:
            col = lax.broadcasted_iota(I32, (CHUNK, BAND), 1) + r0
            valid = jnp.logical_or(jnp.logical_not(first_block), col >= left)
        for p in range(N_PAIRS):
            cols = slice(p * LANES, (p + 1) * LANES)
            q = q_ref[pl.ds(r0, CHUNK), cols]
            k = kw_ref[pl.ds(r0, BAND), cols]
            v = vw_ref[pl.ds(r0, BAND), cols]
            outs = []
            for hh in range(2):
                qm = jnp.where(m0 if hh == 0 else jnp.logical_not(m0), q, jnp.zeros_like(q))
                s = _dot_nt(qm, k) + bias_ref[2 * p + hh]
                if mask_first:
                    s = jnp.where(valid, s, -1e30)
                e = jnp.exp(s - jnp.max(s, axis=-1, keepdims=True))
                pr = (e / jnp.sum(e, axis=-1, keepdims=True)).astype(BF16)
                outs.append(_dot(pr, v))
            o_ref[pl.ds(r0, CHUNK), cols] = jnp.where(m0, outs[0], outs[1]).astype(o_ref.dtype)
        return carry

    lax.fori_loop(0, cpb, chunk, 0)


def _attention(q_src, k_prev_src, kv_cur_src, bias, *, batch, seq, cpb, prev_is_cache):
    left = LEFT_CHUNKS * CHUNK
    rows = cpb * CHUNK
    nblk = seq // rows
    if prev_is_cache:
        prev_idx = lambda col: (lambda b, j: (b, col))
        kcol, vcol = 0, 1
    else:
        assert rows == left
        prev_idx = lambda col: (lambda b, j: (b * nblk + jnp.maximum(j - 1, 0), col))
        kcol, vcol = 1, 2
    cur = lambda col: (lambda b, j: (b * nblk + j, col))
    kern = functools.partial(_attn_kernel, cpb=cpb, mask_first=not prev_is_cache)
    return pl.pallas_call(
        kern,
        grid=(batch, nblk),
        in_specs=[
            pl.BlockSpec((rows, WIDTH), cur(0)),
            pl.BlockSpec((left, WIDTH), prev_idx(kcol)),
            pl.BlockSpec((rows, WIDTH), cur(1)),
            pl.BlockSpec((left, WIDTH), prev_idx(vcol)),
            pl.BlockSpec((rows, WIDTH), cur(2)),
            pl.BlockSpec(bias.shape, lambda b, j: (0, 0, 0)),
        ],
        out_specs=pl.BlockSpec((rows, WIDTH), lambda b, j: (b * nblk + j, 0)),
        out_shape=jax.ShapeDtypeStruct((batch * seq, WIDTH), BF16),
        scratch_shapes=[pltpu.VMEM((left + rows, WIDTH), BF16), pltpu.VMEM((left + rows, WIDTH), BF16)],
        compiler_params=_params(("parallel", "arbitrary")),
        name="attn",
    )(q_src, k_prev_src, kv_cur_src, k_prev_src, kv_cur_src, bias)


def _blockdiag(b, m0):
    z = jnp.zeros_like(b)
    return jnp.concatenate([jnp.where(m0, b, z), jnp.where(m0, z, b)], axis=0).astype(BF16)


def _pairdot(a, b, m0):
    return _dot(a.astype(BF16), _blockdiag(b, m0))


def _gdn_kernel(xbc_ref, gb_ref, ab_ref, convw_ref, alog_ref, dtb_ref, gnorm_ref, s0_ref, tail0_ref,
                bd_ref, eb_ref, eg_ref, lt_ref, eye2_ref,
                ob_ref, sout_ref,
                s_scr, tail_scr, q_scr, k_scr, v_scr, beta_scr, gc_scr, o_scr, *, rows):
    nchunks = rows // CHUNK

    @pl.when(pl.program_id(1) == 0)
    def _():
        s_scr[...] = s0_ref[0]
        tail_scr[...] = tail0_ref[0]

    x = xbc_ref[...]
    xcat = jnp.concatenate([tail_scr[...], x], axis=0)
    w = convw_ref[...]
    y = xcat[CONV_TAIL - 3:CONV_TAIL - 3 + rows] * w[0:1]
    y = y + xcat[CONV_TAIL - 2:CONV_TAIL - 2 + rows] * w[1:2]
    y = y + xcat[CONV_TAIL - 1:CONV_TAIL - 1 + rows] * w[2:3]
    y = y + x * w[3:4]
    tail_scr[...] = x[rows - CONV_TAIL:rows]
    y = y * _sigmoid(y)

    bd = bd_ref[...]
    q = y[:, 0:WIDTH]
    k = y[:, WIDTH:2 * WIDTH]
    q_scr[...] = q * lax.rsqrt(_group_sum(q * q, bd) + RMS_EPS) * (HEAD_DIM ** -0.5)
    k_scr[...] = k * lax.rsqrt(_group_sum(k * k, bd) + RMS_EPS)
    v_scr[...] = y[:, 2 * WIDTH:3 * WIDTH]

    ab = ab_ref[...]
    z = ab + dtb_ref[...]
    softplus = jnp.maximum(z, 0.0) + jnp.log1p(jnp.exp(-jnp.abs(z)))
    g = -jnp.exp(alog_ref[...]) * softplus
    beta_scr[...] = _dot3_rhs01(_sigmoid(ab), eb_ref[...])
    gc_scr[...] = _dot3_lhs01(lt_ref[...], _dot3_rhs01(g, eg_ref[...]))

    ri = lax.broadcasted_iota(I32, (CHUNK, LANES), 0)
    ci = lax.broadcasted_iota(I32, (CHUNK, LANES), 1)
    m0 = ci < HEAD_DIM
    cj = jnp.bitwise_and(ci, HEAD_DIM - 1)
    incl = ri >= cj
    strict = ri > cj
    eye2 = eye2_ref[...]
    ones = jnp.ones((CHUNK, CHUNK), BF16)

    def chunk(c, carry):
        r0 = pl.multiple_of(c * CHUNK, CHUNK)
        rs = pl.ds(r0, CHUNK)
        for p in range(N_PAIRS):
            cols = slice(p * LANES, (p + 1) * LANES)
            qp = q_scr[rs, cols]
            kp = k_scr[rs, cols]
            vp = v_scr[rs, cols]
            beta = beta_scr[rs, cols]
            gc = gc_scr[rs, cols]
            gl = gc[CHUNK - 1:CHUNK, :]
            eg = jnp.exp(gc)
            kb = kp * beta
            bdk = _blockdiag(kp, m0)
            kk = _dot_nt(kb.astype(BF16), bdk)
            qk = _dot_nt(qp.astype(BF16), bdk)
            gcol = _dot3_lhs01(ones, gc * eye2)
            diff = gc - gcol
            decay = jnp.where(incl, jnp.exp(jnp.where(incl, diff, 0.0)), 0.0)
            nmat = -(kk * jnp.where(strict, decay, 0.0))
            t = eye2 + nmat
            pw = nmat
            for _ in range(5):
                pw = _pairdot(pw, pw, m0)
                t = t + _pairdot(t, pw, m0)
            value = _pairdot(t, vp * beta, m0)
            kcum = _pairdot(t, kb * eg, m0)
            s = s_scr[p]
            u = value - _pairdot(kcum, s, m0)
            o = _pairdot(qp * eg, s, m0) + _pairdot(qk * decay, u, m0)
            kd = kp * jnp.exp(gl - gc)
            f = _dot_tn(kd.astype(BF16), u.astype(BF16))
            s_scr[p] = s * jnp.exp(gl) + jnp.where(m0, f[0:CHUNK], f[CHUNK:2 * CHUNK])
            o_scr[rs, cols] = o
        return carry

    lax.fori_loop(0, nchunks, chunk, 0)

    o = o_scr[...]
    on = o * lax.rsqrt(_group_sum(o * o, bd) * (1.0 / HEAD_DIM) + RMS_EPS) * gnorm_ref[...]
    gate = gb_ref[...]
    ob_ref[...] = (on * (gate * _sigmoid(gate))).astype(ob_ref.dtype)
    sout_ref[0] = s_scr[...]


def _gdn(xbc, gb, ab, convw, alog, dtb, gnorm, s0, tail0, consts, *, batch, seq, rows):
    nblk = seq // rows
    bd, eb, eg, lt, eye2 = consts
    blk = lambda b, j: (b * nblk + j, 0)
    c2 = lambda b, j: (0, 0)
    per_b3 = lambda b, j: (b, 0, 0)
    per_b4 = lambda b, j: (b, 0, 0, 0)
    kern = functools.partial(_gdn_kernel, rows=rows)
    wide = pltpu.VMEM((rows, WIDTH), F32)
    return pl.pallas_call(
        kern,
        grid=(batch, nblk),
        in_specs=[
            pl.BlockSpec((rows, 3 * WIDTH), blk),
            pl.BlockSpec((rows, WIDTH), blk),
            pl.BlockSpec((rows, LANES), blk),
            pl.BlockSpec(convw.shape, c2),
            pl.BlockSpec(alog.shape, c2),
            pl.BlockSpec(dtb.shape, c2),
            pl.BlockSpec(gnorm.shape, c2),
            pl.BlockSpec((1, N_PAIRS, HEAD_DIM, LANES), per_b4),
            pl.BlockSpec((1, CONV_TAIL, 3 * WIDTH), per_b3),
            pl.BlockSpec(bd.shape, c2),
            pl.BlockSpec(eb.shape, c2),
            pl.BlockSpec(eg.shape, c2),
            pl.BlockSpec(lt.shape, c2),
            pl.BlockSpec(eye2.shape, c2),
        ],
        out_specs=[
            pl.BlockSpec((rows, WIDTH), blk),
            pl.BlockSpec((1, N_PAIRS, HEAD_DIM, LANES), per_b4),
        ],
        out_shape=[
            jax.ShapeDtypeStruct((batch * seq, WIDTH), BF16),
            jax.ShapeDtypeStruct((batch, N_PAIRS, HEAD_DIM, LANES), F32),
        ],
        scratch_shapes=[
            pltpu.VMEM((N_PAIRS, HEAD_DIM, LANES), F32),
            pltpu.VMEM((CONV_TAIL, 3 * WIDTH), F32),
            wide, wide, wide, wide, wide, wide,
        ],
        compiler_params=_params(("parallel", "arbitrary")),
        name="gdn",
    )(xbc, gb, ab, convw, alog, dtb, gnorm, s0, tail0, bd, eb, eg, lt, eye2)


def _post_kernel(oa_ref, ob_ref, gate_ref, x_ref, wa_ref, wb_ref, wo_ref, nf_ref, wrh_ref, wrl_ref,
                 x1_ref, h2_ref, route_ref):
    d = x_ref.shape[1]
    a = _dot(oa_ref[...], wa_ref[...])
    b = _dot(ob_ref[...], wb_ref[...])
    merged = _sigmoid(gate_ref[:, 0:d]) * a + _sigmoid(gate_ref[:, d:2 * d]) * b
    x1 = x_ref[...] + _dot(merged.astype(BF16), wo_ref[...])
    x1_ref[...] = x1
    h2 = x1 * lax.rsqrt(jnp.mean(x1 * x1, axis=-1, keepdims=True) + RMS_EPS) * nf_ref[...]
    h2_ref[...] = h2

    hi, lo = _split2(h2)
    wrh = wrh_ref[...]
    logits = _dot(hi, wrh) + (_dot(lo, wrh) + _dot(hi, wrl_ref[...]))

    lane = lax.broadcasted_iota(I32, logits.shape, 1).astype(F32)
    big = float(LANES)
    ninf = -jnp.inf
    lg = jnp.where(lane < N_GROUPS, logits, ninf)
    mg = jnp.max(lg, axis=-1, keepdims=True)
    grp = jnp.min(jnp.where(lg == mg, lane, big), axis=-1, keepdims=True)
    p_grp = 1.0 / jnp.sum(jnp.exp(lg - mg), axis=-1, keepdims=True)
    lo_lane = N_GROUPS + grp * EXPERTS_PER_GROUP
    le = jnp.where(jnp.logical_and(lane >= lo_lane, lane < lo_lane + EXPERTS_PER_GROUP), logits, ninf)
    m1 = jnp.max(le, axis=-1, keepdims=True)
    i1 = jnp.min(jnp.where(le == m1, lane, big), axis=-1, keepdims=True)
    le2 = jnp.where(lane == i1, ninf, le)
    m2 = jnp.max(le2, axis=-1, keepdims=True)
    i2 = jnp.min(jnp.where(le2 == m2, lane, big), axis=-1, keepdims=True)
    e2 = jnp.exp(m2 - m1)
    w1 = 1.0 / (1.0 + e2) * p_grp
    w2 = e2 / (1.0 + e2) * p_grp
    out = jnp.where(lane == 0, i1 - N_GROUPS, 0.0)
    out = jnp.where(lane == 1, i2 - N_GROUPS, out)
    out = jnp.where(lane == 2, w1, out)
    out = jnp.where(lane == 3, w2, out)
    route_ref[...] = out


def _post(oa, ob, gate, x, wa, wb, wo, nf, wrh, wrl, tm):
    n, d = x.shape
    row = lambda i: (i, 0)
    const = lambda i: (0, 0)
    return pl.pallas_call(
        _post_kernel,
        grid=(n // tm,),
        in_specs=[
            pl.BlockSpec((tm, WIDTH), row), pl.BlockSpec((tm, WIDTH), row),
            pl.BlockSpec((tm, 2 * d), row), pl.BlockSpec((tm, d), row),
            pl.BlockSpec(wa.shape, const), pl.BlockSpec(wb.shape, const), pl.BlockSpec(wo.shape, const),
            pl.BlockSpec(nf.shape, const), pl.BlockSpec(wrh.shape, const), pl.BlockSpec(wrl.shape, const),
        ],
        out_specs=[pl.BlockSpec((tm, d), row), pl.BlockSpec((tm, d), row), pl.BlockSpec((tm, LANES), row)],
        out_shape=[
            jax.ShapeDtypeStruct((n, d), F32),
            jax.ShapeDtypeStruct((n, d), F32),
            jax.ShapeDtypeStruct((n, LANES), F32),
        ],
        compiler_params=_params(("parallel",)),
        name="post",
    )(oa, ob, gate, x, wa, wb, wo, nf, wrh, wrl)


def _rank_kernel(route_ref, tril_ref, rank_ref, cnt_ref, carry_scr):
    @pl.when(pl.program_id(0) == 0)
    def _():
        carry_scr[...] = jnp.zeros_like(carry_scr)

    r = route_ref[...]
    lane = lax.broadcasted_iota(I32, r.shape, 1)
    lanef = lane.astype(F32)
    oh1 = lanef == r[:, 0:1]
    oh2 = lanef == r[:, 1:2]
    oh = jnp.where(jnp.logical_or(oh1, oh2), 1.0, 0.0)
    before = _dot(tril_ref[...], oh.astype(BF16)) + carry_scr[...]
    rank1 = jnp.sum(jnp.where(oh1, before, 0.0), axis=-1, keepdims=True)
    rank2 = jnp.sum(jnp.where(oh2, before, 0.0), axis=-1, keepdims=True)
    rank_ref[...] = jnp.where(lane == 0, rank1, jnp.where(lane == 1, rank2, 0.0))
    carry_scr[...] = carry_scr[...] + jnp.sum(oh, axis=0, keepdims=True)
    cnt_ref[...] = carry_scr[...]


def _rank(route, tril):
    n = route.shape[0]
    tr = min(tril.shape[0], n)
    tril = tril[:tr, :tr]
    return pl.pallas_call(
        _rank_kernel,
        grid=(n // tr,),
        in_specs=[pl.BlockSpec((tr, LANES), lambda i: (i, 0)), pl.BlockSpec(tril.shape, lambda i: (0, 0))],
        out_specs=[pl.BlockSpec((tr, LANES), lambda i: (i, 0)), pl.BlockSpec((1, LANES), lambda i: (0, 0))],
        out_shape=[jax.ShapeDtypeStruct((n, LANES), F32), jax.ShapeDtypeStruct((1, LANES), F32)],
        scratch_shapes=[pltpu.VMEM((1, LANES), F32)],
        compiler_params=_params(("arbitrary",)),
        name="rank",
    )(route, tril)


def _row_copy(src_ref, src_row, dst_ref, dst_row, sem):
    return pltpu.make_async_copy(src_ref.at[pl.ds(src_row, 1)], dst_ref.at[pl.ds(dst_row, 1)], sem)


def _dispatch_kernel(dest_ref, h_ref, xs_in_ref, xs_ref, sem):
    del xs_in_ref
    td = h_ref.shape[0]

    def issue(t, carry):
        _row_copy(h_ref, t, xs_ref, dest_ref[0, 0, 2 * t], sem).start()
        _row_copy(h_ref, t, xs_ref, dest_ref[0, 0, 2 * t + 1], sem).start()
        return carry

    lax.fori_loop(0, td, issue, 0)

    def drain(t, carry):
        _row_copy(h_ref, 0, xs_ref, 0, sem).wait()
        _row_copy(h_ref, 0, xs_ref, 0, sem).wait()
        return carry

    lax.fori_loop(0, td, drain, 0)


def _dispatch(dest, h2, xs_init, td):
    n, d = h2.shape
    return pl.pallas_call(
        _dispatch_kernel,
        grid=(n // td,),
        in_specs=[
            pl.BlockSpec((1, 1, 2 * td), lambda i: (i, 0, 0), memory_space=pltpu.SMEM),
            pl.BlockSpec((td, d), lambda i: (i, 0)),
            pl.BlockSpec(memory_space=pl.ANY),
        ],
        out_specs=pl.BlockSpec(memory_space=pl.ANY),
        out_shape=jax.ShapeDtypeStruct(xs_init.shape, xs_init.dtype),
        scratch_shapes=[pltpu.SemaphoreType.DMA(())],
        input_output_aliases={2: 0},
        compiler_params=_params(("arbitrary",)),
        name="dispatch",
    )(dest, h2, xs_init)


def _ffn_kernel(blk_e_ref, nused_ref, xs_ref, wg_ref, wu_ref, wd_ref, ys_ref):
    del blk_e_ref
    i = pl.program_id(0)

    @pl.when(i < nused_ref[0])
    def _():
        xb = xs_ref[...].astype(BF16)
        g = _dot(xb, wg_ref[0])
        u = _dot(xb, wu_ref[0])
        hb = (g * _sigmoid(g)) * u
        ys_ref[...] = _dot(hb.astype(BF16), wd_ref[0])

    @pl.when(i >= nused_ref[0])
    def _():
        ys_ref[...] = jnp.zeros_like(ys_ref)


def _ffn(blk_e, nused, xs, wg, wu, wd):
    npad, d = xs.shape
    f = wg.shape[2]
    nb = npad // MOE_ROWS
    grid_spec = pltpu.PrefetchScalarGridSpec(
        num_scalar_prefetch=2,
        grid=(nb,),
        in_specs=[
            pl.BlockSpec((MOE_ROWS, d), lambda i, be, nu: (i, 0)),
            pl.BlockSpec((1, d, f), lambda i, be, nu: (be[i], 0, 0)),
            pl.BlockSpec((1, d, f), lambda i, be, nu: (be[i], 0, 0)),
            pl.BlockSpec((1, f, d), lambda i, be, nu: (be[i], 0, 0)),
        ],
        out_specs=pl.BlockSpec((MOE_ROWS, d), lambda i, be, nu: (i, 0)),
    )
    return pl.pallas_call(
        _ffn_kernel,
        grid_spec=grid_spec,
        out_shape=jax.ShapeDtypeStruct((npad, d), F32),
        compiler_params=_params(("arbitrary",)),
        name="ffn",
    )(blk_e, nused, xs, wg, wu, wd)


def _combine_kernel(dest_ref, x1_ref, route_ref, nfin_ref, ys_ref, out_ref, g1_ref, g2_ref, sem):
    td = x1_ref.shape[0]

    def issue(t, carry):
        _row_copy(ys_ref, dest_ref[0, 0, 2 * t], g1_ref, t, sem).start()
        _row_copy(ys_ref, dest_ref[0, 0, 2 * t + 1], g2_ref, t, sem).start()
        return carry

    lax.fori_loop(0, td, issue, 0)

    def drain(t, carry):
        _row_copy(ys_ref, 0, g1_ref, 0, sem).wait()
        _row_copy(ys_ref, 0, g2_ref, 0, sem).wait()
        return carry

    lax.fori_loop(0, td, drain, 0)

    r = route_ref[...]
    y = x1_ref[...] + (r[:, 2:3] * g1_ref[...] + r[:, 3:4] * g2_ref[...])
    out_ref[...] = y * lax.rsqrt(jnp.mean(y * y, axis=-1, keepdims=True) + RMS_EPS) * nfin_ref[...]


def _combine(dest, x1, route, nfin, ys, td):
    n, d = x1.shape
    return pl.pallas_call(
        _combine_kernel,
        grid=(n // td,),
        in_specs=[
            pl.BlockSpec((1, 1, 2 * td), lambda i: (i, 0, 0), memory_space=pltpu.SMEM),
            pl.BlockSpec((td, d), lambda i: (i, 0)),
            pl.BlockSpec((td, LANES), lambda i: (i, 0)),
            pl.BlockSpec((1, d), lambda i: (0, 0)),
            pl.BlockSpec(memory_space=pl.ANY),
        ],
        out_specs=pl.BlockSpec((td, d), lambda i: (i, 0)),
        out_shape=jax.ShapeDtypeStruct((n, d), F32),
        scratch_shapes=[pltpu.VMEM((td, d), F32), pltpu.VMEM((td, d), F32), pltpu.SemaphoreType.DMA(())],
        compiler_params=_params(("arbitrary",)),
        name="combine",
    )(dest, x1, route, nfin, ys)


def _moe(x1, h2, route, wg, wu, wd, nfin, tril, td):
    n, d = x1.shape
    rank, cnt = _rank(route, tril)
    counts = cnt[0, :N_EXPERTS].astype(I32)
    pcounts = (counts + MOE_ROWS - 1) // MOE_ROWS * MOE_ROWS
    pends = jnp.cumsum(pcounts)
    pstarts = pends - pcounts
    nb = (2 * n) // MOE_ROWS + N_EXPERTS
    blk_e = jnp.minimum(
        jnp.sum(pends[None, :] <= (jnp.arange(nb, dtype=I32) * MOE_ROWS)[:, None], axis=1), N_EXPERTS - 1
    ).astype(I32)
    nused = (pends[-1:] // MOE_ROWS).astype(I32)
    eid = route[:, 0:2].astype(I32)
    dest = (pstarts[eid] + rank[:, 0:2].astype(I32)).reshape(n // td, 1, 2 * td)
    xs = _dispatch(dest, h2, jnp.zeros((nb * MOE_ROWS, d), F32), td)
    ys = _ffn(blk_e, nused, xs, wg, wu, wd)
    return _combine(dest, x1, route, nfin, ys, td)


def _layer(x, past_k, past_v, s0, conv_buf, wts, consts, *, prompt):
    (norm_mix, w_in_parts, bias, conv_w, alog, dtb, gnorm, wa, wb, wo, norm_ffn, wrh, wrl,
     wg, wu, wd, norm_final) = wts
    batch, seq, d = x.shape
    n = batch * seq
    left = LEFT_CHUNKS * CHUNK
    tm = 512 if n % 512 == 0 else n
    xf = x.reshape(n, d)
    qkv, kv, xbc, gb, ab, gate = _inproj(xf, norm_mix, w_in_parts, tm)

    if prompt:
        o_a = _attention(qkv, qkv, qkv, bias, batch=batch, seq=seq, cpb=LEFT_CHUNKS, prev_is_cache=False)
        rows = left
        s_init = jnp.zeros((batch, N_PAIRS, HEAD_DIM, LANES), F32)
        tail = jnp.zeros((batch, CONV_TAIL, 3 * WIDTH), F32)
    else:
        cache = jnp.concatenate(
            [past_k.reshape(batch * left, WIDTH), past_v.reshape(batch * left, WIDTH)], axis=1).astype(BF16)
        o_a = _attention(qkv, cache, qkv, bias, batch=batch, seq=seq, cpb=seq // CHUNK, prev_is_cache=True)
        rows = seq
        s_init = s0.reshape(batch, N_PAIRS, 2, HEAD_DIM, HEAD_DIM).transpose(0, 1, 3, 2, 4).reshape(
            batch, N_PAIRS, HEAD_DIM, LANES)
        tail = jnp.pad(conv_buf, ((0, 0), (CONV_TAIL - (CONV_W - 1), 0), (0, 0)))

    bd, eb, eg, lt_by_rows, eye2, tril = consts
    o_b, s_new = _gdn(xbc, gb, ab, conv_w, alog, dtb, gnorm, s_init, tail,
                      (bd, eb, eg, lt_by_rows[rows], eye2), batch=batch, seq=seq, rows=rows)

    x1, h2, route = _post(o_a, o_b, gate, xf, wa, wb, wo, norm_ffn, wrh, wrl, tm)
    y = _moe(x1, h2, route, wg, wu, wd, norm_final, tril, td=min(256, n))

    keep = min(left, seq)
    kv4 = kv.reshape(batch, seq, 2, N_HEADS, HEAD_DIM)
    new_k = kv4[:, seq - keep:, 0]
    new_v = kv4[:, seq - keep:, 1]
    new_s = s_new.reshape(batch, N_PAIRS, HEAD_DIM, 2, HEAD_DIM).transpose(0, 1, 3, 2, 4).reshape(
        batch, N_HEADS, HEAD_DIM, HEAD_DIM)
    new_conv = xbc.reshape(batch, seq, 3 * WIDTH)[:, seq - (CONV_W - 1):]
    return y.reshape(batch, seq, d), new_k, new_v, new_s, new_conv


def _constants(row_sizes):
    lane = jnp.arange(WIDTH)
    bd = (lane[:, None] // HEAD_DIM == lane[None, :] // HEAD_DIM).astype(BF16)
    src = jnp.arange(LANES)
    eb = (src[:, None] == lane[None, :] // HEAD_DIM).astype(BF16)
    eg = (src[:, None] == N_HEADS + lane[None, :] // HEAD_DIM).astype(BF16)
    lt = {}
    for r in row_sizes:
        i = jnp.arange(r)
        lt[r] = jnp.logical_and(i[:, None] >= i[None, :], i[:, None] // CHUNK == i[None, :] // CHUNK).astype(BF16)
    c = jnp.arange(CHUNK)
    eye2 = (c[:, None] == jnp.arange(LANES)[None, :] % HEAD_DIM).astype(F32)
    t = jnp.arange(512)
    tril = (t[:, None] > t[None, :]).astype(BF16)
    return bd, eb, eg, lt, eye2, tril


def kernel(x_prompt, x_sample, cache_attn_k, cache_attn_v, state_delta, state_conv, norm_mix, w_in, rel_bias,
           conv_w, a_log, dt_bias, gdn_norm, w_branch_a, w_branch_b, w_out, norm_ffn, w_route_group,
           w_route_expert, w_gate, w_up, w_down, norm_final):
    depth = w_in.shape[0]
    d = x_prompt.shape[-1]
    left = LEFT_CHUNKS * CHUNK
    consts = _constants({left, x_sample.shape[1]})
    qi = jnp.arange(CHUNK)
    kj = jnp.arange(BAND)
    dist = jnp.clip(qi[:, None] - kj[None, :] + left, -REL_CLIP, REL_CLIP) + REL_CLIP

    xp, xs = x_prompt, x_sample
    outs_p, outs_s = [], []
    for l in range(depth):
        w = w_in[l]
        o = 0
        parts = []
        for width in (3 * WIDTH, 3 * WIDTH, WIDTH, 2 * N_HEADS, 2 * d):
            parts.append(w[:, o:o + width])
            o += width
        wqkv, wxbc, wgb, wab, wgate = parts
        wqkv = jnp.concatenate([wqkv[:, :WIDTH] * (HEAD_DIM ** -0.5), wqkv[:, WIDTH:]], axis=1)
        wab = jnp.pad(wab, ((0, 0), (0, LANES - 2 * N_HEADS)))
        w_in_parts = tuple(m.astype(BF16) for m in (wqkv, wxbc, wgb, wab, wgate))
        lane_pad = lambda v: jnp.pad(v[None, :], ((0, 0), (N_HEADS, LANES - 2 * N_HEADS)))
        wr = jnp.pad(jnp.concatenate([w_route_group[l], w_route_expert[l]], axis=1),
                     ((0, 0), (0, LANES - N_GROUPS - N_EXPERTS)))
        wrh = wr.astype(BF16)
        wrl = (wr - wrh.astype(F32)).astype(BF16)
        wts = (
            norm_mix[l][None, :], w_in_parts, rel_bias[l][:, dist], conv_w[l], lane_pad(a_log[l]),
            lane_pad(dt_bias[l]), jnp.tile(gdn_norm[l], N_HEADS)[None, :],
            w_branch_a[l].astype(BF16), w_branch_b[l].astype(BF16), w_out[l].astype(BF16),
            norm_ffn[l][None, :], wrh, wrl,
            w_gate[l].astype(BF16), w_up[l].astype(BF16), w_down[l].astype(BF16), norm_final[None, :],
        )
        assert depth == 1
        xp, kp, vp, sp, cp = _layer(xp, None, None, None, None, wts, consts, prompt=True)
        xs, ks, vs, ss, cs = _layer(xs, cache_attn_k[l], cache_attn_v[l], state_delta[l], state_conv[l],
                                    wts, consts, prompt=False)
        outs_p.append((kp, vp, sp, cp))
        outs_s.append((ks, vs, ss, cs))
    stack = lambda items, idx: jnp.stack([it[idx] for it in items])
    return (xp, xs,
            stack(outs_p, 0), stack(outs_p, 1), stack(outs_p, 2), stack(outs_p, 3),
            stack(outs_s, 0), stack(outs_s, 1), stack(outs_s, 2), stack(outs_s, 3))
```

```python
import functools

import jax
import jax.numpy as jnp
from jax import lax
from jax.experimental import pallas as pl
from jax.experimental.pallas import tpu as pltpu

F32 = jnp.float32
BF16 = jnp.bfloat16
I32 = jnp.int32

RMS_EPS = 1e-6
CHUNK = 64
LEFT_CHUNKS = 8
BAND = (LEFT_CHUNKS + 1) * CHUNK
N_HEADS = 8
HEAD_DIM = 64
N_PAIRS = N_HEADS // 2
WIDTH = N_HEADS * HEAD_DIM
REL_CLIP = 128
CONV_W = 4
N_GROUPS = 4
EXPERTS_PER_GROUP = 8
N_EXPERTS = N_GROUPS * EXPERTS_PER_GROUP
LANES = 128
CONV_TAIL = 8
MOE_ROWS = 256
ROW_UNROLL = 8
VMEM_LIMIT = 56 * 1024 * 1024


def _dot(a, b):
    return jnp.dot(a, b, preferred_element_type=F32)


def _dot_nt(a, b):
    return lax.dot_general(a, b, (((1,), (1,)), ((), ())), preferred_element_type=F32)


def _dot_tn(a, b):
    return lax.dot_general(a, b, (((0,), (0,)), ((), ())), preferred_element_type=F32)


def _sigmoid(x):
    return 1.0 / (1.0 + jnp.exp(-x))


def _split2(x):
    hi = x.astype(BF16)
    lo = (x - hi.astype(F32)).astype(BF16)
    return hi, lo


def _split3(x):
    hi = x.astype(BF16)
    r = x - hi.astype(F32)
    mid = r.astype(BF16)
    lo = (r - mid.astype(F32)).astype(BF16)
    return hi, mid, lo


def _dot3_rhs01(x, mat):
    hi, mid, lo = _split3(x)
    return (_dot(hi, mat) + _dot(mid, mat)) + _dot(lo, mat)


def _dot3_lhs01(mat, x):
    hi, mid, lo = _split3(x)
    return (_dot(mat, hi) + _dot(mat, mid)) + _dot(mat, lo)


def _group_sum(x2, bd):
    hi, lo = _split2(x2)
    return _dot(hi, bd) + _dot(lo, bd)


def _params(sem):
    return pltpu.CompilerParams(dimension_semantics=sem, vmem_limit_bytes=VMEM_LIMIT)


def _inproj_kernel(x_ref, g_ref, wqkv_ref, wxbc_ref, wgb_ref, wab_ref, wgate_ref,
                   qkv_ref, kv_ref, xbc_ref, gb_ref, ab_ref, gate_ref):
    x = x_ref[...]
    h = x * lax.rsqrt(jnp.mean(x * x, axis=-1, keepdims=True) + RMS_EPS) * g_ref[...]
    hb = h.astype(BF16)
    qkv = _dot(hb, wqkv_ref[...])
    qkv_ref[...] = qkv.astype(BF16)
    kv_ref[...] = qkv[:, WIDTH:]
    xbc_ref[...] = _dot(hb, wxbc_ref[...])
    gb_ref[...] = _dot(hb, wgb_ref[...])
    ab_ref[...] = _dot(hb, wab_ref[...])
    gate_ref[...] = _dot(hb, wgate_ref[...])


def _inproj(x, g, w, tm):
    n, d = x.shape
    wqkv, wxbc, wgb, wab, wgate = w
    row = lambda i: (i, 0)
    const = lambda i: (0, 0)
    widths = (3 * WIDTH, 2 * WIDTH, 3 * WIDTH, WIDTH, LANES, 2 * d)
    dtypes = (BF16, F32, F32, F32, F32, F32)
    return pl.pallas_call(
        _inproj_kernel,
        grid=(n // tm,),
        in_specs=[pl.BlockSpec((tm, d), row), pl.BlockSpec((1, d), const)]
        + [pl.BlockSpec(m.shape, const) for m in w],
        out_specs=[pl.BlockSpec((tm, c), row) for c in widths],
        out_shape=[jax.ShapeDtypeStruct((n, c), t) for c, t in zip(widths, dtypes)],
        compiler_params=_params(("parallel",)),
        name="inproj",
    )(x, g, wqkv, wxbc, wgb, wab, wgate)


def _attn_kernel(q_ref, kp_ref, kc_ref, vp_ref, vc_ref, bias_ref, o_ref, kw_ref, vw_ref, s_ref, p_ref,
                 *, cpb, mask_first):
    left = LEFT_CHUNKS * CHUNK
    kw_ref[0:left, :] = kp_ref[...]
    kw_ref[left:, :] = kc_ref[...]
    vw_ref[0:left, :] = vp_ref[...]
    vw_ref[left:, :] = vc_ref[...]
    lane = lax.broadcasted_iota(I32, (CHUNK, LANES), 1)
    m0 = lane < HEAD_DIM
    first_block = pl.program_id(1) == 0

    def chunk(i, carry):
        r0 = pl.multiple_of(i * CHUNK, CHUNK)
        if mask_first:
            col = lax.broadcasted_iota(I32, (CHUNK, BAND), 1) + r0
            valid = jnp.logical_or(jnp.logical_not(first_block), col >= left)
        for p in range(N_PAIRS):
            cols = slice(p * LANES, (p + 1) * LANES)
            q = q_ref[pl.ds(r0, CHUNK), cols]
            k = kw_ref[pl.ds(r0, BAND), cols]
            for hh in range(2):
                qm = jnp.where(m0 if hh == 0 else jnp.logical_not(m0), q, jnp.zeros_like(q))
                s = _dot_nt(qm, k) + bias_ref[2 * p + hh]
                if mask_first:
                    s = jnp.where(valid, s, -1e30)
                s_ref[2 * p + hh] = s
        for h in range(N_HEADS):
            s = s_ref[h]
            e = jnp.exp(s - jnp.max(s, axis=-1, keepdims=True))
            p_ref[h] = (e / jnp.sum(e, axis=-1, keepdims=True)).astype(BF16)
        pairs = []
        for p in range(N_PAIRS):
            v = vw_ref[pl.ds(r0, BAND), p * LANES:(p + 1) * LANES]
            pairs.append(jnp.where(m0, _dot(p_ref[2 * p], v), _dot(p_ref[2 * p + 1], v)))
        o_ref[pl.ds(r0, CHUNK), :] = jnp.concatenate(pairs, axis=1).astype(o_ref.dtype)
        return carry

    lax.fori_loop(0, cpb, chunk, 0)


def _attention(q_src, k_prev_src, kv_cur_src, bias, *, batch, seq, cpb, prev_is_cache):
    left = LEFT_CHUNKS * CHUNK
    rows = cpb * CHUNK
    nblk = seq // rows
    if prev_is_cache:
        prev_idx = lambda col: (lambda b, j: (b, col))
        kcol, vcol = 0, 1
    else:
        assert rows == left
        prev_idx = lambda col: (lambda b, j: (b * nblk + jnp.maximum(j - 1, 0), col))
        kcol, vcol = 1, 2
    cur = lambda col: (lambda b, j: (b * nblk + j, col))
    kern = functools.partial(_attn_kernel, cpb=cpb, mask_first=not prev_is_cache)
    return pl.pallas_call(
        kern,
        grid=(batch, nblk),
        in_specs=[
            pl.BlockSpec((rows, WIDTH), cur(0)),
            pl.BlockSpec((left, WIDTH), prev_idx(kcol)),
            pl.BlockSpec((rows, WIDTH), cur(1)),
            pl.BlockSpec((left, WIDTH), prev_idx(vcol)),
            pl.BlockSpec((rows, WIDTH), cur(2)),
            pl.BlockSpec(bias.shape, lambda b, j: (0, 0, 0)),
        ],
        out_specs=pl.BlockSpec((rows, WIDTH), lambda b, j: (b * nblk + j, 0)),
        out_shape=jax.ShapeDtypeStruct((batch * seq, WIDTH), BF16),
        scratch_shapes=[pltpu.VMEM((left + rows, WIDTH), BF16), pltpu.VMEM((left + rows, WIDTH), BF16),
                        pltpu.VMEM((N_HEADS, CHUNK, BAND), F32), pltpu.VMEM((N_HEADS, CHUNK, BAND), BF16)],
        compiler_params=_params(("parallel", "arbitrary")),
        name="attn",
    )(q_src, k_prev_src, kv_cur_src, k_prev_src, kv_cur_src, bias)


def _blockdiag(b, m0):
    z = jnp.zeros_like(b)
    return jnp.concatenate([jnp.where(m0, b, z), jnp.where(m0, z, b)], axis=0).astype(BF16)


def _pairdot(a, b, m0):
    return _dot(a.astype(BF16), _blockdiag(b, m0))


def _gdn_kernel(xbc_ref, gb_ref, ab_ref, convw_ref, alog_ref, dtb_ref, gnorm_ref, s0_ref, tail0_ref,
                bd_ref, eb_ref, eg_ref, lt_ref, eye2_ref,
                ob_ref, sout_ref,
                s_scr, tail_scr, xc_scr, kdt_scr, u_scr, q_scr, k_scr, v_scr, beta_scr, gc_scr, o_scr, pw_scr, t_scr, aqk_scr,
                *, rows):
    nchunks = rows // CHUNK

    ri = lax.broadcasted_iota(I32, (CHUNK, LANES), 0)
    ci = lax.broadcasted_iota(I32, (CHUNK, LANES), 1)
    m0 = ci < HEAD_DIM

    @pl.when(pl.program_id(1) == 0)
    def _():
        for p in range(N_PAIRS):
            s = s0_ref[0, p]
            z = jnp.zeros_like(s)
            s_scr[p] = jnp.concatenate([jnp.where(m0, s, z), jnp.where(m0, z, s)], axis=0)
        tail_scr[...] = tail0_ref[0]

    x = xbc_ref[...]
    xc_scr[0:CONV_TAIL, :] = tail_scr[...]
    xc_scr[CONV_TAIL:, :] = x
    w = convw_ref[...]
    y = xc_scr[CONV_TAIL - 3:CONV_TAIL - 3 + rows, :] * w[0:1]
    y = y + xc_scr[CONV_TAIL - 2:CONV_TAIL - 2 + rows, :] * w[1:2]
    y = y + xc_scr[CONV_TAIL - 1:CONV_TAIL - 1 + rows, :] * w[2:3]
    y = y + x * w[3:4]
    tail_scr[...] = x[rows - CONV_TAIL:rows]
    y = y * _sigmoid(y)

    bd = bd_ref[...]
    q = y[:, 0:WIDTH]
    k = y[:, WIDTH:2 * WIDTH]
    q_scr[...] = q * lax.rsqrt(_group_sum(q * q, bd) + RMS_EPS) * (HEAD_DIM ** -0.5)
    k_scr[...] = k * lax.rsqrt(_group_sum(k * k, bd) + RMS_EPS)
    v_scr[...] = y[:, 2 * WIDTH:3 * WIDTH]

    ab = ab_ref[...]
    z = ab + dtb_ref[...]
    softplus = jnp.maximum(z, 0.0) + jnp.log1p(jnp.exp(-jnp.abs(z)))
    g = -jnp.exp(alog_ref[...]) * softplus
    beta_scr[...] = _dot3_rhs01(_sigmoid(ab), eb_ref[...])
    gc_scr[...] = _dot3_lhs01(lt_ref[...], _dot3_rhs01(g, eg_ref[...]))

    cj = jnp.bitwise_and(ci, HEAD_DIM - 1)
    incl = ri >= cj
    strict = ri > cj
    eye2 = eye2_ref[...]
    ones = jnp.ones((CHUNK, CHUNK), BF16)
    r2 = lax.broadcasted_iota(I32, (LANES, LANES), 0)
    c2 = lax.broadcasted_iota(I32, (LANES, LANES), 1)
    eye128 = jnp.where(r2 == c2, 1.0, 0.0).astype(BF16)
    same_head = (r2 < HEAD_DIM) == (c2 < HEAD_DIM)

    def tiles(body, unroll):
        def step(c, carry):
            if isinstance(c, int):
                rs = slice(c * CHUNK, (c + 1) * CHUNK)
            else:
                rs = pl.ds(pl.multiple_of(c * CHUNK, CHUNK), CHUNK)
            for p in range(N_PAIRS):
                body(rs, slice(p * LANES, (p + 1) * LANES), c * N_PAIRS + p, p)
            return carry
        if nchunks <= unroll:
            for c in range(nchunks):
                step(c, 0)
        else:
            lax.fori_loop(0, nchunks, step, 0, unroll=unroll)

    def setup(rs, cols, tile, p):
        qp = q_scr[rs, cols]
        kp = k_scr[rs, cols]
        beta = beta_scr[rs, cols]
        gc = gc_scr[rs, cols]
        gl = gc[CHUNK - 1:CHUNK, :]
        eg = jnp.exp(gc)
        kb = kp * beta
        bdk = _blockdiag(kp, m0)
        kk = _dot_nt(kb.astype(BF16), bdk)
        qk = _dot_nt(qp.astype(BF16), bdk)
        gcol = _dot3_lhs01(ones, gc * eye2)
        decay = jnp.where(incl, jnp.exp(jnp.where(incl, gc - gcol, 0.0)), 0.0)
        nmat = -(kk * jnp.where(strict, decay, 0.0))
        pw_scr[rs, cols] = nmat
        t_scr[rs, cols] = eye2 + nmat
        aqk_scr[rs, cols] = qk * decay
        q_scr[rs, cols] = qp * eg
        kd = (kp * jnp.exp(gl - gc)).astype(BF16)
        kdt_scr[tile] = _dot_nt(eye128, kd).astype(BF16)
        v_scr[rs, cols] = v_scr[rs, cols] * beta
        beta_scr[rs, cols] = kb * eg

    tiles(setup, 2)

    def double(rs, cols, tile, p):
        pw = pw_scr[rs, cols]
        pw = _pairdot(pw, pw, m0)
        pw_scr[rs, cols] = pw
        t = t_scr[rs, cols]
        t_scr[rs, cols] = t + _pairdot(t, pw, m0)

    for _ in range(5):
        tiles(double, 8)

    def solve(rs, cols, tile, p):
        t = t_scr[rs, cols]
        v_scr[rs, cols] = _pairdot(t, v_scr[rs, cols], m0)
        beta_scr[rs, cols] = _pairdot(t, beta_scr[rs, cols], m0)

    tiles(solve, 8)

    def scan_u(rs, cols, tile, p):
        u_scr[p] = v_scr[rs, cols] - _dot(beta_scr[rs, cols].astype(BF16), s_scr[p].astype(BF16))

    def scan_s(rs, cols, tile, p):
        s = s_scr[p]
        u = u_scr[p]
        o_scr[rs, cols] = (_dot(q_scr[rs, cols].astype(BF16), s.astype(BF16))
                           + _pairdot(aqk_scr[rs, cols], u, m0))
        f = _dot(kdt_scr[tile], u.astype(BF16))
        gt = jnp.exp(gc_scr[rs, cols][CHUNK - 1:CHUNK, :])
        s_scr[p] = s * gt + jnp.where(same_head, f, 0.0)

    def scan(c, carry):
        rs = pl.ds(pl.multiple_of(c * CHUNK, CHUNK), CHUNK)
        for body in (scan_u, scan_s):
            for p in range(N_PAIRS):
                body(rs, slice(p * LANES, (p + 1) * LANES), c * N_PAIRS + p, p)
        return carry

    lax.fori_loop(0, nchunks, scan, 0)

    o = o_scr[...]
    on = o * lax.rsqrt(_group_sum(o * o, bd) * (1.0 / HEAD_DIM) + RMS_EPS) * gnorm_ref[...]
    gate = gb_ref[...]
    ob_ref[...] = (on * (gate * _sigmoid(gate))).astype(ob_ref.dtype)
    for p in range(N_PAIRS):
        s = s_scr[p]
        sout_ref[0, p] = jnp.where(m0, s[0:HEAD_DIM], s[HEAD_DIM:2 * HEAD_DIM])


def _gdn(xbc, gb, ab, convw, alog, dtb, gnorm, s0, tail0, consts, *, batch, seq, rows):
    nblk = seq // rows
    bd, eb, eg, lt, eye2 = consts
    blk = lambda b, j: (b * nblk + j, 0)
    c2 = lambda b, j: (0, 0)
    per_b3 = lambda b, j: (b, 0, 0)
    per_b4 = lambda b, j: (b, 0, 0, 0)
    kern = functools.partial(_gdn_kernel, rows=rows)
    wide = pltpu.VMEM((rows, WIDTH), F32)
    return pl.pallas_call(
        kern,
        grid=(batch, nblk),
        in_specs=[
            pl.BlockSpec((rows, 3 * WIDTH), blk),
            pl.BlockSpec((rows, WIDTH), blk),
            pl.BlockSpec((rows, LANES), blk),
            pl.BlockSpec(convw.shape, c2),
            pl.BlockSpec(alog.shape, c2),
            pl.BlockSpec(dtb.shape, c2),
            pl.BlockSpec(gnorm.shape, c2),
            pl.BlockSpec((1, N_PAIRS, HEAD_DIM, LANES), per_b4),
            pl.BlockSpec((1, CONV_TAIL, 3 * WIDTH), per_b3),
            pl.BlockSpec(bd.shape, c2),
            pl.BlockSpec(eb.shape, c2),
            pl.BlockSpec(eg.shape, c2),
            pl.BlockSpec(lt.shape, c2),
            pl.BlockSpec(eye2.shape, c2),
        ],
        out_specs=[
            pl.BlockSpec((rows, WIDTH), blk),
            pl.BlockSpec((1, N_PAIRS, HEAD_DIM, LANES), per_b4),
        ],
        out_shape=[
            jax.ShapeDtypeStruct((batch * seq, WIDTH), BF16),
            jax.ShapeDtypeStruct((batch, N_PAIRS, HEAD_DIM, LANES), F32),
        ],
        scratch_shapes=[
            pltpu.VMEM((N_PAIRS, LANES, LANES), F32),
            pltpu.VMEM((CONV_TAIL, 3 * WIDTH), F32),
            pltpu.VMEM((CONV_TAIL + rows, 3 * WIDTH), F32),
            pltpu.VMEM((rows // CHUNK * N_PAIRS, LANES, HEAD_DIM), BF16),
            pltpu.VMEM((N_PAIRS, CHUNK, LANES), F32),
            wide, wide, wide, wide, wide, wide, wide, wide, wide,
        ],
        compiler_params=_params(("parallel", "arbitrary")),
        name="gdn",
    )(xbc, gb, ab, convw, alog, dtb, gnorm, s0, tail0, bd, eb, eg, lt, eye2)


def _post_kernel(oa_ref, ob_ref, gate_ref, x_ref, wa_ref, wb_ref, wo_ref, nf_ref, wrh_ref, wrl_ref,
                 x1_ref, h2_ref, route_ref):
    d = x_ref.shape[1]
    a = _dot(oa_ref[...], wa_ref[...])
    b = _dot(ob_ref[...], wb_ref[...])
    merged = _sigmoid(gate_ref[:, 0:d]) * a + _sigmoid(gate_ref[:, d:2 * d]) * b
    x1 = x_ref[...] + _dot(merged.astype(BF16), wo_ref[...])
    x1_ref[...] = x1
    h2 = x1 * lax.rsqrt(jnp.mean(x1 * x1, axis=-1, keepdims=True) + RMS_EPS) * nf_ref[...]
    h2_ref[...] = h2

    hi, lo = _split2(h2)
    wrh = wrh_ref[...]
    logits = _dot(hi, wrh) + (_dot(lo, wrh) + _dot(hi, wrl_ref[...]))

    lane = lax.broadcasted_iota(I32, logits.shape, 1).astype(F32)
    big = float(LANES)
    ninf = -jnp.inf
    lg = jnp.where(lane < N_GROUPS, logits, ninf)
    mg = jnp.max(lg, axis=-1, keepdims=True)
    grp = jnp.min(jnp.where(lg == mg, lane, big), axis=-1, keepdims=True)
    p_grp = 1.0 / jnp.sum(jnp.exp(lg - mg), axis=-1, keepdims=True)
    lo_lane = N_GROUPS + grp * EXPERTS_PER_GROUP
    le = jnp.where(jnp.logical_and(lane >= lo_lane, lane < lo_lane + EXPERTS_PER_GROUP), logits, ninf)
    m1 = jnp.max(le, axis=-1, keepdims=True)
    i1 = jnp.min(jnp.where(le == m1, lane, big), axis=-1, keepdims=True)
    le2 = jnp.where(lane == i1, ninf, le)
    m2 = jnp.max(le2, axis=-1, keepdims=True)
    i2 = jnp.min(jnp.where(le2 == m2, lane, big), axis=-1, keepdims=True)
    e2 = jnp.exp(m2 - m1)
    w1 = 1.0 / (1.0 + e2) * p_grp
    w2 = e2 / (1.0 + e2) * p_grp
    out = jnp.where(lane == 0, i1 - N_GROUPS, 0.0)
    out = jnp.where(lane == 1, i2 - N_GROUPS, out)
    out = jnp.where(lane == 2, w1, out)
    out = jnp.where(lane == 3, w2, out)
    route_ref[...] = out


def _post(oa, ob, gate, x, wa, wb, wo, nf, wrh, wrl, tm):
    n, d = x.shape
    row = lambda i: (i, 0)
    const = lambda i: (0, 0)
    return pl.pallas_call(
        _post_kernel,
        grid=(n // tm,),
        in_specs=[
            pl.BlockSpec((tm, WIDTH), row), pl.BlockSpec((tm, WIDTH), row),
            pl.BlockSpec((tm, 2 * d), row), pl.BlockSpec((tm, d), row),
            pl.BlockSpec(wa.shape, const), pl.BlockSpec(wb.shape, const), pl.BlockSpec(wo.shape, const),
            pl.BlockSpec(nf.shape, const), pl.BlockSpec(wrh.shape, const), pl.BlockSpec(wrl.shape, const),
        ],
        out_specs=[pl.BlockSpec((tm, d), row), pl.BlockSpec((tm, d), row), pl.BlockSpec((tm, LANES), row)],
        out_shape=[
            jax.ShapeDtypeStruct((n, d), F32),
            jax.ShapeDtypeStruct((n, d), F32),
            jax.ShapeDtypeStruct((n, LANES), F32),
        ],
        compiler_params=_params(("parallel",)),
        name="post",
    )(oa, ob, gate, x, wa, wb, wo, nf, wrh, wrl)


def _rank_kernel(route_ref, tril_ref, rank_ref, cnt_ref, carry_scr):
    @pl.when(pl.program_id(0) == 0)
    def _():
        carry_scr[...] = jnp.zeros_like(carry_scr)

    r = route_ref[...]
    lane = lax.broadcasted_iota(I32, r.shape, 1)
    lanef = lane.astype(F32)
    oh1 = lanef == r[:, 0:1]
    oh2 = lanef == r[:, 1:2]
    oh = jnp.where(jnp.logical_or(oh1, oh2), 1.0, 0.0)
    before = _dot(tril_ref[...], oh.astype(BF16)) + carry_scr[...]
    rank1 = jnp.sum(jnp.where(oh1, before, 0.0), axis=-1, keepdims=True)
    rank2 = jnp.sum(jnp.where(oh2, before, 0.0), axis=-1, keepdims=True)
    rank_ref[...] = jnp.where(lane == 0, rank1, jnp.where(lane == 1, rank2, 0.0))
    carry_scr[...] = carry_scr[...] + jnp.sum(oh, axis=0, keepdims=True)
    cnt_ref[...] = carry_scr[...]


def _rank(route, tril):
    n = route.shape[0]
    tr = min(tril.shape[0], n)
    tril = tril[:tr, :tr]
    return pl.pallas_call(
        _rank_kernel,
        grid=(n // tr,),
        in_specs=[pl.BlockSpec((tr, LANES), lambda i: (i, 0)), pl.BlockSpec(tril.shape, lambda i: (0, 0))],
        out_specs=[pl.BlockSpec((tr, LANES), lambda i: (i, 0)), pl.BlockSpec((1, LANES), lambda i: (0, 0))],
        out_shape=[jax.ShapeDtypeStruct((n, LANES), F32), jax.ShapeDtypeStruct((1, LANES), F32)],
        scratch_shapes=[pltpu.VMEM((1, LANES), F32)],
        compiler_params=_params(("arbitrary",)),
        name="rank",
    )(route, tril)


def _row_copy(src_ref, src_row, dst_ref, dst_row, sem):
    return pltpu.make_async_copy(src_ref.at[pl.ds(src_row, 1)], dst_ref.at[pl.ds(dst_row, 1)], sem)


def _for_rows(td, per_row):
    def group(i, carry):
        for j in range(ROW_UNROLL):
            per_row(i * ROW_UNROLL + j, j)
        return carry
    lax.fori_loop(0, td // ROW_UNROLL, group, 0)


def _dispatch_kernel(dest_ref, h_ref, xs_in_ref, xs_ref, sem):
    del xs_in_ref
    td = h_ref.shape[0]

    def issue(t, j):
        _row_copy(h_ref, t, xs_ref, dest_ref[0, 0, 2 * t], sem).start(priority=0)
        _row_copy(h_ref, t, xs_ref, dest_ref[0, 0, 2 * t + 1], sem).start(priority=1)

    _for_rows(td, issue)
    whole = pltpu.make_async_copy(h_ref, xs_ref.at[pl.ds(0, td)], sem)
    whole.wait()
    whole.wait()


def _dispatch(dest, h2, xs_init, td):
    n, d = h2.shape
    return pl.pallas_call(
        _dispatch_kernel,
        grid=(n // td,),
        in_specs=[
            pl.BlockSpec((1, 1, 2 * td), lambda i: (i, 0, 0), memory_space=pltpu.SMEM),
            pl.BlockSpec((td, d), lambda i: (i, 0)),
            pl.BlockSpec(memory_space=pl.ANY),
        ],
        out_specs=pl.BlockSpec(memory_space=pl.ANY),
        out_shape=jax.ShapeDtypeStruct(xs_init.shape, xs_init.dtype),
        scratch_shapes=[pltpu.SemaphoreType.DMA(())],
        input_output_aliases={2: 0},
        compiler_params=_params(("arbitrary",)),
        name="dispatch",
    )(dest, h2, xs_init)


def _ffn_kernel(blk_e_ref, nused_ref, xs_ref, wg_ref, wu_ref, wd_ref, ys_ref):
    del blk_e_ref
    i = pl.program_id(0)

    @pl.when(i < nused_ref[0])
    def _():
        xb = xs_ref[...].astype(BF16)
        g = _dot(xb, wg_ref[0])
        u = _dot(xb, wu_ref[0])
        hb = (g * _sigmoid(g)) * u
        ys_ref[...] = _dot(hb.astype(BF16), wd_ref[0])

    @pl.when(i >= nused_ref[0])
    def _():
        ys_ref[...] = jnp.zeros_like(ys_ref)


def _ffn(blk_e, nused, xs, wg, wu, wd):
    npad, d = xs.shape
    f = wg.shape[2]
    nb = npad // MOE_ROWS
    grid_spec = pltpu.PrefetchScalarGridSpec(
        num_scalar_prefetch=2,
        grid=(nb,),
        in_specs=[
            pl.BlockSpec((MOE_ROWS, d), lambda i, be, nu: (i, 0)),
            pl.BlockSpec((1, d, f), lambda i, be, nu: (be[i], 0, 0)),
            pl.BlockSpec((1, d, f), lambda i, be, nu: (be[i], 0, 0)),
            pl.BlockSpec((1, f, d), lambda i, be, nu: (be[i], 0, 0)),
        ],
        out_specs=pl.BlockSpec((MOE_ROWS, d), lambda i, be, nu: (i, 0)),
    )
    return pl.pallas_call(
        _ffn_kernel,
        grid_spec=grid_spec,
        out_shape=jax.ShapeDtypeStruct((npad, d), F32),
        compiler_params=_params(("arbitrary",)),
        name="ffn",
    )(blk_e, nused, xs, wg, wu, wd)


def _combine_kernel(dest_ref, x1_ref, route_ref, nfin_ref, ys_ref, out_ref, g1_ref, g2_ref, sem):
    td = x1_ref.shape[0]

    def issue(t, j):
        _row_copy(ys_ref, dest_ref[0, 0, 2 * t], g1_ref, t, sem).start(priority=0)
        _row_copy(ys_ref, dest_ref[0, 0, 2 * t + 1], g2_ref, t, sem).start(priority=1)

    _for_rows(td, issue)
    pltpu.make_async_copy(ys_ref.at[pl.ds(0, td)], g1_ref, sem).wait()
    pltpu.make_async_copy(ys_ref.at[pl.ds(0, td)], g2_ref, sem).wait()

    r = route_ref[...]
    y = x1_ref[...] + (r[:, 2:3] * g1_ref[...] + r[:, 3:4] * g2_ref[...])
    out_ref[...] = y * lax.rsqrt(jnp.mean(y * y, axis=-1, keepdims=True) + RMS_EPS) * nfin_ref[...]


def _combine(dest, x1, route, nfin, ys, td):
    n, d = x1.shape
    return pl.pallas_call(
        _combine_kernel,
        grid=(n // td,),
        in_specs=[
            pl.BlockSpec((1, 1, 2 * td), lambda i: (i, 0, 0), memory_space=pltpu.SMEM),
            pl.BlockSpec((td, d), lambda i: (i, 0)),
            pl.BlockSpec((td, LANES), lambda i: (i, 0)),
            pl.BlockSpec((1, d), lambda i: (0, 0)),
            pl.BlockSpec(memory_space=pl.ANY),
        ],
        out_specs=pl.BlockSpec((td, d), lambda i: (i, 0)),
        out_shape=jax.ShapeDtypeStruct((n, d), F32),
        scratch_shapes=[pltpu.VMEM((td, d), F32), pltpu.VMEM((td, d), F32), pltpu.SemaphoreType.DMA(())],
        compiler_params=_params(("arbitrary",)),
        name="combine",
    )(dest, x1, route, nfin, ys)


def _moe(x1, h2, route, wg, wu, wd, nfin, tril, td):
    n, d = x1.shape
    rank, cnt = _rank(route, tril)
    counts = cnt[0, :N_EXPERTS].astype(I32)
    pcounts = (counts + MOE_ROWS - 1) // MOE_ROWS * MOE_ROWS
    pends = jnp.cumsum(pcounts)
    pstarts = pends - pcounts
    nb = (2 * n) // MOE_ROWS + N_EXPERTS
    blk_e = jnp.minimum(
        jnp.sum(pends[None, :] <= (jnp.arange(nb, dtype=I32) * MOE_ROWS)[:, None], axis=1), N_EXPERTS - 1
    ).astype(I32)
    nused = (pends[-1:] // MOE_ROWS).astype(I32)
    eid = route[:, 0:2].astype(I32)
    dest = (pstarts[eid] + rank[:, 0:2].astype(I32)).reshape(n // td, 1, 2 * td)
    xs = _dispatch(dest, h2, jnp.zeros((nb * MOE_ROWS, d), F32), td)
    ys = _ffn(blk_e, nused, xs, wg, wu, wd)
    return _combine(dest, x1, route, nfin, ys, td)


def _layer(x, past_k, past_v, s0, conv_buf, wts, consts, *, prompt):
    (norm_mix, w_in_parts, bias, conv_w, alog, dtb, gnorm, wa, wb, wo, norm_ffn, wrh, wrl,
     wg, wu, wd, norm_final) = wts
    batch, seq, d = x.shape
    n = batch * seq
    left = LEFT_CHUNKS * CHUNK
    tm = 512 if n % 512 == 0 else n
    xf = x.reshape(n, d)
    qkv, kv, xbc, gb, ab, gate = _inproj(xf, norm_mix, w_in_parts, tm)

    if prompt:
        o_a = _attention(qkv, qkv, qkv, bias, batch=batch, seq=seq, cpb=LEFT_CHUNKS, prev_is_cache=False)
        rows = left
        s_init = jnp.zeros((batch, N_PAIRS, HEAD_DIM, LANES), F32)
        tail = jnp.zeros((batch, CONV_TAIL, 3 * WIDTH), F32)
    else:
        cache = jnp.concatenate(
            [past_k.reshape(batch * left, WIDTH), past_v.reshape(batch * left, WIDTH)], axis=1).astype(BF16)
        o_a = _attention(qkv, cache, qkv, bias, batch=batch, seq=seq, cpb=seq // CHUNK, prev_is_cache=True)
        rows = seq
        s_init = s0.reshape(batch, N_PAIRS, 2, HEAD_DIM, HEAD_DIM).transpose(0, 1, 3, 2, 4).reshape(
            batch, N_PAIRS, HEAD_DIM, LANES)
        tail = jnp.pad(conv_buf, ((0, 0), (CONV_TAIL - (CONV_W - 1), 0), (0, 0)))

    bd, eb, eg, lt_by_rows, eye2, tril = consts
    o_b, s_new = _gdn(xbc, gb, ab, conv_w, alog, dtb, gnorm, s_init, tail,
                      (bd, eb, eg, lt_by_rows[rows], eye2), batch=batch, seq=seq, rows=rows)

    x1, h2, route = _post(o_a, o_b, gate, xf, wa, wb, wo, norm_ffn, wrh, wrl, tm)
    y = _moe(x1, h2, route, wg, wu, wd, norm_final, tril, td=min(512, n))

    keep = min(left, seq)
    kv4 = kv.reshape(batch, seq, 2, N_HEADS, HEAD_DIM)
    new_k = kv4[:, seq - keep:, 0]
    new_v = kv4[:, seq - keep:, 1]
    new_s = s_new.reshape(batch, N_PAIRS, HEAD_DIM, 2, HEAD_DIM).transpose(0, 1, 3, 2, 4).reshape(
        batch, N_HEADS, HEAD_DIM, HEAD_DIM)
    new_conv = xbc.reshape(batch, seq, 3 * WIDTH)[:, seq - (CONV_W - 1):]
    return y.reshape(batch, seq, d), new_k, new_v, new_s, new_conv


def _constants(row_sizes):
    lane = jnp.arange(WIDTH)
    bd = (lane[:, None] // HEAD_DIM == lane[None, :] // HEAD_DIM).astype(BF16)
    src = jnp.arange(LANES)
    eb = (src[:, None] == lane[None, :] // HEAD_DIM).astype(BF16)
    eg = (src[:, None] == N_HEADS + lane[None, :] // HEAD_DIM).astype(BF16)
    lt = {}
    for r in row_sizes:
        i = jnp.arange(r)
        lt[r] = jnp.logical_and(i[:, None] >= i[None, :], i[:, None] // CHUNK == i[None, :] // CHUNK).astype(BF16)
    c = jnp.arange(CHUNK)
    eye2 = (c[:, None] == jnp.arange(LANES)[None, :] % HEAD_DIM).astype(F32)
    t = jnp.arange(512)
    tril = (t[:, None] > t[None, :]).astype(BF16)
    return bd, eb, eg, lt, eye2, tril


def kernel(x_prompt, x_sample, cache_attn_k, cache_attn_v, state_delta, state_conv, norm_mix, w_in, rel_bias,
           conv_w, a_log, dt_bias, gdn_norm, w_branch_a, w_branch_b, w_out, norm_ffn, w_route_group,
           w_route_expert, w_gate, w_up, w_down, norm_final):
    depth = w_in.shape[0]
    d = x_prompt.shape[-1]
    left = LEFT_CHUNKS * CHUNK
    consts = _constants({left, x_sample.shape[1]})
    qi = jnp.arange(CHUNK)
    kj = jnp.arange(BAND)
    dist = jnp.clip(qi[:, None] - kj[None, :] + left, -REL_CLIP, REL_CLIP) + REL_CLIP

    xp, xs = x_prompt, x_sample
    outs_p, outs_s = [], []
    for l in range(depth):
        w = w_in[l]
        o = 0
        parts = []
        for width in (3 * WIDTH, 3 * WIDTH, WIDTH, 2 * N_HEADS, 2 * d):
            parts.append(w[:, o:o + width])
            o += width
        wqkv, wxbc, wgb, wab, wgate = parts
        wqkv = jnp.concatenate([wqkv[:, :WIDTH] * (HEAD_DIM ** -0.5), wqkv[:, WIDTH:]], axis=1)
        wab = jnp.pad(wab, ((0, 0), (0, LANES - 2 * N_HEADS)))
        w_in_parts = tuple(m.astype(BF16) for m in (wqkv, wxbc, wgb, wab, wgate))
        lane_pad = lambda v: jnp.pad(v[None, :], ((0, 0), (N_HEADS, LANES - 2 * N_HEADS)))
        wr = jnp.pad(jnp.concatenate([w_route_group[l], w_route_expert[l]], axis=1),
                     ((0, 0), (0, LANES - N_GROUPS - N_EXPERTS)))
        wrh = wr.astype(BF16)
        wrl = (wr - wrh.astype(F32)).astype(BF16)
        wts = (
            norm_mix[l][None, :], w_in_parts, rel_bias[l][:, dist], conv_w[l], lane_pad(a_log[l]),
            lane_pad(dt_bias[l]), jnp.tile(gdn_norm[l], N_HEADS)[None, :],
            w_branch_a[l].astype(BF16), w_branch_b[l].astype(BF16), w_out[l].astype(BF16),
            norm_ffn[l][None, :], wrh, wrl,
            w_gate[l].astype(BF16), w_up[l].astype(BF16), w_down[l].astype(BF16), norm_final[None, :],
        )
        assert depth == 1
        xp, kp, vp, sp, cp = _layer(xp, None, None, None, None, wts, consts, prompt=True)
        xs, ks, vs, ss, cs = _layer(xs, cache_attn_k[l], cache_attn_v[l], state_delta[l], state_conv[l],
                                    wts, consts, prompt=False)
        outs_p.append((kp, vp, sp, cp))
        outs_s.append((ks, vs, ss, cs))
    stack = lambda items, idx: jnp.stack([it[idx] for it in items])
    return (xp, xs,
            stack(outs_p, 0), stack(outs_p, 1), stack(outs_p, 2), stack(outs_p, 3),
            stack(outs_s, 0), stack(outs_s, 1), stack(outs_s, 2), stack(outs_s, 3))
```

```python
import functools

import jax
import jax.numpy as jnp
from jax import lax
from jax.experimental import pallas as pl
from jax.experimental.pallas import tpu as pltpu

F32 = jnp.float32
BF16 = jnp.bfloat16
I32 = jnp.int32

RMS_EPS = 1e-6
CHUNK = 64
LEFT_CHUNKS = 8
BAND = (LEFT_CHUNKS + 1) * CHUNK
N_HEADS = 8
HEAD_DIM = 64
N_PAIRS = N_HEADS // 2
WIDTH = N_HEADS * HEAD_DIM
REL_CLIP = 128
CONV_W = 4
N_GROUPS = 4
EXPERTS_PER_GROUP = 8
N_EXPERTS = N_GROUPS * EXPERTS_PER_GROUP
LANES = 128
CONV_TAIL = 8
MOE_ROWS = 512
MOE_ROWS_SMALL = 128
ROW_UNROLL = 8
VMEM_LIMIT = 56 * 1024 * 1024


def _dot(a, b):
    return jnp.dot(a, b, preferred_element_type=F32)


def _dot_nt(a, b):
    return lax.dot_general(a, b, (((1,), (1,)), ((), ())), preferred_element_type=F32)


def _dot_tn(a, b):
    return lax.dot_general(a, b, (((0,), (0,)), ((), ())), preferred_element_type=F32)


def _sigmoid(x):
    return 1.0 / (1.0 + jnp.exp(-x))


def _split2(x):
    hi = x.astype(BF16)
    lo = (x - hi.astype(F32)).astype(BF16)
    return hi, lo


def _split3(x):
    hi = x.astype(BF16)
    r = x - hi.astype(F32)
    mid = r.astype(BF16)
    lo = (r - mid.astype(F32)).astype(BF16)
    return hi, mid, lo


def _dot3_rhs01(x, mat):
    hi, mid, lo = _split3(x)
    return (_dot(hi, mat) + _dot(mid, mat)) + _dot(lo, mat)


def _dot3_lhs01(mat, x):
    hi, mid, lo = _split3(x)
    return (_dot(mat, hi) + _dot(mat, mid)) + _dot(mat, lo)


def _group_sum(x2, bd):
    hi, lo = _split2(x2)
    outs = []
    for p in range(x2.shape[1] // LANES):
        cols = slice(p * LANES, (p + 1) * LANES)
        outs.append(_dot(hi[:, cols], bd) + _dot(lo[:, cols], bd))
    return jnp.concatenate(outs, axis=1)


def _params(sem):
    return pltpu.CompilerParams(dimension_semantics=sem, vmem_limit_bytes=VMEM_LIMIT)


def _inproj_kernel(x_ref, g_ref, wqkv_ref, wxbc_ref, wgb_ref, wab_ref, wgate_ref,
                   qkv_ref, kv_ref, xbc_ref, gb_ref, ab_ref, gate_ref, *, tiles_per_seq):
    x = x_ref[...]
    h = x * lax.rsqrt(jnp.mean(x * x, axis=-1, keepdims=True) + RMS_EPS) * g_ref[...]
    hb = h.astype(BF16)
    qkv_ref[...] = _dot(hb, wqkv_ref[...]).astype(BF16)

    @pl.when(pl.program_id(0) % tiles_per_seq == tiles_per_seq - 1)
    def _():
        kv_ref[...] = _dot(hb, wqkv_ref[:, WIDTH:])

    xbc_ref[...] = _dot(hb, wxbc_ref[...])
    gb_ref[...] = _dot(hb, wgb_ref[...])
    ab_ref[...] = _dot(hb, wab_ref[...])
    gate_ref[...] = _dot(hb, wgate_ref[...])


def _inproj(x, g, w, tm, tiles_per_seq):
    n, d = x.shape
    wqkv, wxbc, wgb, wab, wgate = w
    row = lambda i: (i, 0)
    tail = lambda i: (i // tiles_per_seq, 0)
    const = lambda i: (0, 0)
    widths = (3 * WIDTH, 2 * WIDTH, 3 * WIDTH, WIDTH, LANES, 2 * d)
    dtypes = (BF16, F32, F32, F32, F32, F32)
    nrows = (n, n // tiles_per_seq, n, n, n, n)
    maps = (row, tail, row, row, row, row)
    return pl.pallas_call(
        functools.partial(_inproj_kernel, tiles_per_seq=tiles_per_seq),
        grid=(n // tm,),
        in_specs=[pl.BlockSpec((tm, d), row), pl.BlockSpec((1, d), const)]
        + [pl.BlockSpec(m.shape, const) for m in w],
        out_specs=[pl.BlockSpec((tm, c), m) for c, m in zip(widths, maps)],
        out_shape=[jax.ShapeDtypeStruct((r, c), t) for r, c, t in zip(nrows, widths, dtypes)],
        compiler_params=_params(("arbitrary",)),
        name="inproj",
    )(x, g, wqkv, wxbc, wgb, wab, wgate)


def _attn_kernel(q_ref, kp_ref, kc_ref, vp_ref, vc_ref, bias_ref, o_ref, kw_ref, vw_ref, s_ref, p_ref,
                 *, cpb, mask_first):
    left = LEFT_CHUNKS * CHUNK
    kw_ref[0:left, :] = kp_ref[...]
    kw_ref[left:, :] = kc_ref[...]
    vw_ref[0:left, :] = vp_ref[...]
    vw_ref[left:, :] = vc_ref[...]
    lane = lax.broadcasted_iota(I32, (CHUNK, LANES), 1)
    m0 = lane < HEAD_DIM
    first_block = pl.program_id(1) == 0

    def chunk(i, carry):
        r0 = pl.multiple_of(i * CHUNK, CHUNK)
        if mask_first:
            col = lax.broadcasted_iota(I32, (CHUNK, BAND), 1) + r0
            valid = jnp.logical_or(jnp.logical_not(first_block), col >= left)
        for p in range(N_PAIRS):
            cols = slice(p * LANES, (p + 1) * LANES)
            q = q_ref[pl.ds(r0, CHUNK), cols]
            k = kw_ref[pl.ds(r0, BAND), cols]
            for hh in range(2):
                qm = jnp.where(m0 if hh == 0 else jnp.logical_not(m0), q, jnp.zeros_like(q))
                s = _dot_nt(qm, k) + bias_ref[2 * p + hh]
                if mask_first:
                    s = jnp.where(valid, s, -1e30)
                s_ref[2 * p + hh] = s
        for h in range(N_HEADS):
            s = s_ref[h]
            e = jnp.exp(s - jnp.max(s, axis=-1, keepdims=True))
            p_ref[h] = (e / jnp.sum(e, axis=-1, keepdims=True)).astype(BF16)
        pairs = []
        for p in range(N_PAIRS):
            v = vw_ref[pl.ds(r0, BAND), p * LANES:(p + 1) * LANES]
            pairs.append(jnp.where(m0, _dot(p_ref[2 * p], v), _dot(p_ref[2 * p + 1], v)))
        o_ref[pl.ds(r0, CHUNK), :] = jnp.concatenate(pairs, axis=1).astype(o_ref.dtype)
        return carry

    lax.fori_loop(0, cpb, chunk, 0)


def _attention(q_src, k_prev_src, kv_cur_src, bias, *, batch, seq, cpb, prev_is_cache):
    left = LEFT_CHUNKS * CHUNK
    rows = cpb * CHUNK
    nblk = seq // rows
    if prev_is_cache:
        prev_idx = lambda col: (lambda b, j: (b, col))
        kcol, vcol = 0, 1
    else:
        assert rows == left
        prev_idx = lambda col: (lambda b, j: (b * nblk + jnp.maximum(j - 1, 0), col))
        kcol, vcol = 1, 2
    cur = lambda col: (lambda b, j: (b * nblk + j, col))
    kern = functools.partial(_attn_kernel, cpb=cpb, mask_first=not prev_is_cache)
    return pl.pallas_call(
        kern,
        grid=(batch, nblk),
        in_specs=[
            pl.BlockSpec((rows, WIDTH), cur(0)),
            pl.BlockSpec((left, WIDTH), prev_idx(kcol)),
            pl.BlockSpec((rows, WIDTH), cur(1)),
            pl.BlockSpec((left, WIDTH), prev_idx(vcol)),
            pl.BlockSpec((rows, WIDTH), cur(2)),
            pl.BlockSpec(bias.shape, lambda b, j: (0, 0, 0)),
        ],
        out_specs=pl.BlockSpec((rows, WIDTH), lambda b, j: (b * nblk + j, 0)),
        out_shape=jax.ShapeDtypeStruct((batch * seq, WIDTH), BF16),
        scratch_shapes=[pltpu.VMEM((left + rows, WIDTH), BF16), pltpu.VMEM((left + rows, WIDTH), BF16),
                        pltpu.VMEM((N_HEADS, CHUNK, BAND), F32), pltpu.VMEM((N_HEADS, CHUNK, BAND), BF16)],
        compiler_params=_params(("parallel", "arbitrary")),
        name="attn",
    )(q_src, k_prev_src, kv_cur_src, k_prev_src, kv_cur_src, bias)


def _blockdiag(b, m0):
    z = jnp.zeros_like(b)
    return jnp.concatenate([jnp.where(m0, b, z), jnp.where(m0, z, b)], axis=0).astype(BF16)


def _pairdot(a, b, m0):
    return _dot(a.astype(BF16), _blockdiag(b, m0))


def _gdn_kernel(xbc_ref, gb_ref, ab_ref, convw_ref, alog_ref, dtb_ref, gnorm_ref, s0_ref, tail0_ref,
                bd_ref, eb_ref, eg_ref, lt_ref, eye2_ref,
                ob_ref, sout_ref,
                s_scr, tail_scr, xc_scr, kdt_scr, u_scr, q_scr, k_scr, v_scr, beta_scr, gc_scr, o_scr, pw_scr, t_scr, aqk_scr,
                *, rows):
    nchunks = rows // CHUNK

    ri = lax.broadcasted_iota(I32, (CHUNK, LANES), 0)
    ci = lax.broadcasted_iota(I32, (CHUNK, LANES), 1)
    m0 = ci < HEAD_DIM

    @pl.when(pl.program_id(1) == 0)
    def _():
        for p in range(N_PAIRS):
            s = s0_ref[0, p]
            z = jnp.zeros_like(s)
            s_scr[p] = jnp.concatenate([jnp.where(m0, s, z), jnp.where(m0, z, s)], axis=0)
        tail_scr[...] = tail0_ref[0]

    x = xbc_ref[...]
    xc_scr[0:CONV_TAIL, :] = tail_scr[...]
    xc_scr[CONV_TAIL:, :] = x
    w = convw_ref[...]
    y = xc_scr[CONV_TAIL - 3:CONV_TAIL - 3 + rows, :] * w[0:1]
    y = y + xc_scr[CONV_TAIL - 2:CONV_TAIL - 2 + rows, :] * w[1:2]
    y = y + xc_scr[CONV_TAIL - 1:CONV_TAIL - 1 + rows, :] * w[2:3]
    y = y + x * w[3:4]
    tail_scr[...] = x[rows - CONV_TAIL:rows]
    y = y * _sigmoid(y)

    bd = bd_ref[...]
    q = y[:, 0:WIDTH]
    k = y[:, WIDTH:2 * WIDTH]
    q_scr[...] = q * lax.rsqrt(_group_sum(q * q, bd) + RMS_EPS) * (HEAD_DIM ** -0.5)
    k_scr[...] = k * lax.rsqrt(_group_sum(k * k, bd) + RMS_EPS)
    v_scr[...] = y[:, 2 * WIDTH:3 * WIDTH]

    ab = ab_ref[...]
    z = ab + dtb_ref[...]
    softplus = jnp.maximum(z, 0.0) + jnp.log1p(jnp.exp(-jnp.abs(z)))
    g = -jnp.exp(alog_ref[...]) * softplus
    beta_scr[...] = _dot3_rhs01(_sigmoid(ab), eb_ref[...])
    gexp = _dot3_rhs01(g, eg_ref[...])
    lt = lt_ref[...]
    for c in range(nchunks):
        gc_scr[c * CHUNK:(c + 1) * CHUNK, :] = _dot3_lhs01(lt, gexp[c * CHUNK:(c + 1) * CHUNK])

    cj = jnp.bitwise_and(ci, HEAD_DIM - 1)
    incl = ri >= cj
    strict = ri > cj
    eye2 = eye2_ref[...]
    ones = jnp.ones((CHUNK, CHUNK), BF16)
    r2 = lax.broadcasted_iota(I32, (LANES, LANES), 0)
    c2 = lax.broadcasted_iota(I32, (LANES, LANES), 1)
    eye128 = jnp.where(r2 == c2, 1.0, 0.0).astype(BF16)
    same_head = (r2 < HEAD_DIM) == (c2 < HEAD_DIM)

    def tiles(body, unroll):
        def step(c, carry):
            if isinstance(c, int):
                rs = slice(c * CHUNK, (c + 1) * CHUNK)
            else:
                rs = pl.ds(pl.multiple_of(c * CHUNK, CHUNK), CHUNK)
            for p in range(N_PAIRS):
                body(rs, slice(p * LANES, (p + 1) * LANES), c * N_PAIRS + p, p)
            return carry
        if nchunks <= unroll:
            for c in range(nchunks):
                step(c, 0)
        else:
            lax.fori_loop(0, nchunks, step, 0, unroll=unroll)

    def setup(rs, cols, tile, p):
        qp = q_scr[rs, cols]
        kp = k_scr[rs, cols]
        beta = beta_scr[rs, cols]
        gc = gc_scr[rs, cols]
        gl = gc[CHUNK - 1:CHUNK, :]
        eg = jnp.exp(gc)
        kb = kp * beta
        bdk = _blockdiag(kp, m0)
        kk = _dot_nt(kb.astype(BF16), bdk)
        qk = _dot_nt(qp.astype(BF16), bdk)
        gcol = _dot3_lhs01(ones, gc * eye2)
        decay = jnp.where(incl, jnp.exp(jnp.where(incl, gc - gcol, 0.0)), 0.0)
        nmat = -(kk * jnp.where(strict, decay, 0.0))
        pw_scr[rs, cols] = nmat
        t_scr[rs, cols] = eye2 + nmat
        aqk_scr[rs, cols] = qk * decay
        q_scr[rs, cols] = qp * eg
        kd = (kp * jnp.exp(gl - gc)).astype(BF16)
        kdt_scr[tile] = _dot_nt(eye128, kd).astype(BF16)
        v_scr[rs, cols] = v_scr[rs, cols] * beta
        beta_scr[rs, cols] = kb * eg

    tiles(setup, 8)

    def double(rs, cols, tile, p):
        pw = pw_scr[rs, cols]
        pw = _pairdot(pw, pw, m0)
        pw_scr[rs, cols] = pw
        t = t_scr[rs, cols]
        t_scr[rs, cols] = t + _pairdot(t, pw, m0)

    for _ in range(5):
        tiles(double, 8)

    def solve(rs, cols, tile, p):
        t = t_scr[rs, cols]
        v_scr[rs, cols] = _pairdot(t, v_scr[rs, cols], m0)
        beta_scr[rs, cols] = _pairdot(t, beta_scr[rs, cols], m0)

    tiles(solve, 8)

    def scan_u(rs, cols, tile, p):
        u_scr[p] = v_scr[rs, cols] - _dot(beta_scr[rs, cols].astype(BF16), s_scr[p].astype(BF16))

    def scan_s(rs, cols, tile, p):
        s = s_scr[p]
        u = u_scr[p]
        o_scr[rs, cols] = (_dot(q_scr[rs, cols].astype(BF16), s.astype(BF16))
                           + _pairdot(aqk_scr[rs, cols], u, m0))
        f = _dot(kdt_scr[tile], u.astype(BF16))
        gt = jnp.exp(gc_scr[rs, cols][CHUNK - 1:CHUNK, :])
        s_scr[p] = s * gt + jnp.where(same_head, f, 0.0)

    def scan(c, carry):
        rs = pl.ds(pl.multiple_of(c * CHUNK, CHUNK), CHUNK)
        for body in (scan_u, scan_s):
            for p in range(N_PAIRS):
                body(rs, slice(p * LANES, (p + 1) * LANES), c * N_PAIRS + p, p)
        return carry

    lax.fori_loop(0, nchunks, scan, 0)

    o = o_scr[...]
    on = o * lax.rsqrt(_group_sum(o * o, bd) * (1.0 / HEAD_DIM) + RMS_EPS) * gnorm_ref[...]
    gate = gb_ref[...]
    ob_ref[...] = (on * (gate * _sigmoid(gate))).astype(ob_ref.dtype)
    for p in range(N_PAIRS):
        s = s_scr[p]
        sout_ref[0, p] = jnp.where(m0, s[0:HEAD_DIM], s[HEAD_DIM:2 * HEAD_DIM])


def _gdn(xbc, gb, ab, convw, alog, dtb, gnorm, s0, tail0, consts, *, batch, seq, rows):
    nblk = seq // rows
    bd, eb, eg, lt, eye2 = consts
    blk = lambda b, j: (b * nblk + j, 0)
    c2 = lambda b, j: (0, 0)
    per_b3 = lambda b, j: (b, 0, 0)
    per_b4 = lambda b, j: (b, 0, 0, 0)
    kern = functools.partial(_gdn_kernel, rows=rows)
    wide = pltpu.VMEM((rows, WIDTH), F32)
    return pl.pallas_call(
        kern,
        grid=(batch, nblk),
        in_specs=[
            pl.BlockSpec((rows, 3 * WIDTH), blk),
            pl.BlockSpec((rows, WIDTH), blk),
            pl.BlockSpec((rows, LANES), blk),
            pl.BlockSpec(convw.shape, c2),
            pl.BlockSpec(alog.shape, c2),
            pl.BlockSpec(dtb.shape, c2),
            pl.BlockSpec(gnorm.shape, c2),
            pl.BlockSpec((1, N_PAIRS, HEAD_DIM, LANES), per_b4),
            pl.BlockSpec((1, CONV_TAIL, 3 * WIDTH), per_b3),
            pl.BlockSpec(bd.shape, c2),
            pl.BlockSpec(eb.shape, c2),
            pl.BlockSpec(eg.shape, c2),
            pl.BlockSpec(lt.shape, c2),
            pl.BlockSpec(eye2.shape, c2),
        ],
        out_specs=[
            pl.BlockSpec((rows, WIDTH), blk),
            pl.BlockSpec((1, N_PAIRS, HEAD_DIM, LANES), per_b4),
        ],
        out_shape=[
            jax.ShapeDtypeStruct((batch * seq, WIDTH), BF16),
            jax.ShapeDtypeStruct((batch, N_PAIRS, HEAD_DIM, LANES), F32),
        ],
        scratch_shapes=[
            pltpu.VMEM((N_PAIRS, LANES, LANES), F32),
            pltpu.VMEM((CONV_TAIL, 3 * WIDTH), F32),
            pltpu.VMEM((CONV_TAIL + rows, 3 * WIDTH), F32),
            pltpu.VMEM((rows // CHUNK * N_PAIRS, LANES, HEAD_DIM), BF16),
            pltpu.VMEM((N_PAIRS, CHUNK, LANES), F32),
            wide, wide, wide, wide, wide, wide, wide, wide, wide,
        ],
        compiler_params=_params(("parallel", "arbitrary")),
        name="gdn",
    )(xbc, gb, ab, convw, alog, dtb, gnorm, s0, tail0, bd, eb, eg, lt, eye2)


def _post_kernel(oa_ref, ob_ref, gate_ref, x_ref, wa_ref, wb_ref, wo_ref, nf_ref, wrh_ref, wrl_ref,
                 x1_ref, h2_ref, route_ref):
    d = x_ref.shape[1]
    a = _dot(oa_ref[...], wa_ref[...])
    b = _dot(ob_ref[...], wb_ref[...])
    merged = _sigmoid(gate_ref[:, 0:d]) * a + _sigmoid(gate_ref[:, d:2 * d]) * b
    x1 = x_ref[...] + _dot(merged.astype(BF16), wo_ref[...])
    x1_ref[...] = x1
    h2 = x1 * lax.rsqrt(jnp.mean(x1 * x1, axis=-1, keepdims=True) + RMS_EPS) * nf_ref[...]
    h2_ref[...] = h2

    hi, lo = _split2(h2)
    wrh = wrh_ref[...]
    logits = _dot(hi, wrh) + (_dot(lo, wrh) + _dot(hi, wrl_ref[...]))

    lane = lax.broadcasted_iota(I32, logits.shape, 1).astype(F32)
    big = float(LANES)
    ninf = -jnp.inf
    lg = jnp.where(lane < N_GROUPS, logits, ninf)
    mg = jnp.max(lg, axis=-1, keepdims=True)
    grp = jnp.min(jnp.where(lg == mg, lane, big), axis=-1, keepdims=True)
    p_grp = 1.0 / jnp.sum(jnp.exp(lg - mg), axis=-1, keepdims=True)
    lo_lane = N_GROUPS + grp * EXPERTS_PER_GROUP
    le = jnp.where(jnp.logical_and(lane >= lo_lane, lane < lo_lane + EXPERTS_PER_GROUP), logits, ninf)
    m1 = jnp.max(le, axis=-1, keepdims=True)
    i1 = jnp.min(jnp.where(le == m1, lane, big), axis=-1, keepdims=True)
    le2 = jnp.where(lane == i1, ninf, le)
    m2 = jnp.max(le2, axis=-1, keepdims=True)
    i2 = jnp.min(jnp.where(le2 == m2, lane, big), axis=-1, keepdims=True)
    e2 = jnp.exp(m2 - m1)
    w1 = 1.0 / (1.0 + e2) * p_grp
    w2 = e2 / (1.0 + e2) * p_grp
    out = jnp.where(lane == 0, i1 - N_GROUPS, 0.0)
    out = jnp.where(lane == 1, i2 - N_GROUPS, out)
    out = jnp.where(lane == 2, w1, out)
    out = jnp.where(lane == 3, w2, out)
    route_ref[...] = out


def _post(oa, ob, gate, x, wa, wb, wo, nf, wrh, wrl, tm):
    n, d = x.shape
    row = lambda i: (i, 0)
    const = lambda i: (0, 0)
    return pl.pallas_call(
        _post_kernel,
        grid=(n // tm,),
        in_specs=[
            pl.BlockSpec((tm, WIDTH), row), pl.BlockSpec((tm, WIDTH), row),
            pl.BlockSpec((tm, 2 * d), row), pl.BlockSpec((tm, d), row),
            pl.BlockSpec(wa.shape, const), pl.BlockSpec(wb.shape, const), pl.BlockSpec(wo.shape, const),
            pl.BlockSpec(nf.shape, const), pl.BlockSpec(wrh.shape, const), pl.BlockSpec(wrl.shape, const),
        ],
        out_specs=[pl.BlockSpec((tm, d), row), pl.BlockSpec((tm, d), row), pl.BlockSpec((tm, LANES), row)],
        out_shape=[
            jax.ShapeDtypeStruct((n, d), F32),
            jax.ShapeDtypeStruct((n, d), F32),
            jax.ShapeDtypeStruct((n, LANES), F32),
        ],
        compiler_params=_params(("parallel",)),
        name="post",
    )(oa, ob, gate, x, wa, wb, wo, nf, wrh, wrl)


def _rank_kernel(route_ref, tril_ref, rank_ref, cnt_ref, carry_scr):
    @pl.when(pl.program_id(0) == 0)
    def _():
        carry_scr[...] = jnp.zeros_like(carry_scr)

    r = route_ref[...]
    lane = lax.broadcasted_iota(I32, r.shape, 1)
    lanef = lane.astype(F32)
    oh1 = lanef == r[:, 0:1]
    oh2 = lanef == r[:, 1:2]
    oh = jnp.where(jnp.logical_or(oh1, oh2), 1.0, 0.0)
    before = _dot(tril_ref[...], oh.astype(BF16)) + carry_scr[...]
    rank1 = jnp.sum(jnp.where(oh1, before, 0.0), axis=-1, keepdims=True)
    rank2 = jnp.sum(jnp.where(oh2, before, 0.0), axis=-1, keepdims=True)
    rank_ref[...] = jnp.where(lane == 0, rank1, jnp.where(lane == 1, rank2, 0.0))
    carry_scr[...] = carry_scr[...] + jnp.sum(oh, axis=0, keepdims=True)
    cnt_ref[...] = carry_scr[...]


def _rank(route, tril):
    n = route.shape[0]
    tr = min(tril.shape[0], n)
    tril = tril[:tr, :tr]
    return pl.pallas_call(
        _rank_kernel,
        grid=(n // tr,),
        in_specs=[pl.BlockSpec((tr, LANES), lambda i: (i, 0)), pl.BlockSpec(tril.shape, lambda i: (0, 0))],
        out_specs=[pl.BlockSpec((tr, LANES), lambda i: (i, 0)), pl.BlockSpec((1, LANES), lambda i: (0, 0))],
        out_shape=[jax.ShapeDtypeStruct((n, LANES), F32), jax.ShapeDtypeStruct((1, LANES), F32)],
        scratch_shapes=[pltpu.VMEM((1, LANES), F32)],
        compiler_params=_params(("arbitrary",)),
        name="rank",
    )(route, tril)


def _dest_kernel(route_ref, rank_ref, pstart_ref, dest_ref):
    r = route_ref[...]
    rk = rank_ref[...]
    ps = pstart_ref[...]
    lane = lax.broadcasted_iota(I32, r.shape, 1)
    lanef = lane.astype(F32)
    d1 = jnp.sum(jnp.where(lanef == r[:, 0:1], ps, 0.0), axis=-1, keepdims=True) + rk[:, 0:1]
    d2 = jnp.sum(jnp.where(lanef == r[:, 1:2], ps, 0.0), axis=-1, keepdims=True) + rk[:, 1:2]
    dest_ref[...] = jnp.where(lane == 0, d1, jnp.where(lane == 1, d2, 0.0)).astype(I32)


def _dest(route, rank, pstart, tr):
    n = route.shape[0]
    row = lambda i: (i, 0)
    return pl.pallas_call(
        _dest_kernel,
        grid=(n // tr,),
        in_specs=[pl.BlockSpec((tr, LANES), row), pl.BlockSpec((tr, LANES), row),
                  pl.BlockSpec((1, LANES), lambda i: (0, 0))],
        out_specs=pl.BlockSpec((tr, LANES), row),
        out_shape=jax.ShapeDtypeStruct((n, LANES), I32),
        compiler_params=_params(("parallel",)),
        name="dest",
    )(route, rank, pstart)


def _row_copy(src_ref, src_row, dst_ref, dst_row, sem):
    return pltpu.make_async_copy(src_ref.at[pl.ds(src_row, 1)], dst_ref.at[pl.ds(dst_row, 1)], sem)


def _for_rows(td, per_row):
    def group(i, carry):
        for j in range(ROW_UNROLL):
            per_row(i * ROW_UNROLL + j, j)
        return carry
    lax.fori_loop(0, td // ROW_UNROLL, group, 0)


def _dispatch_kernel(dest_ref, h_ref, xs_in_ref, xs_ref, sem):
    del xs_in_ref
    td = h_ref.shape[0]

    def issue(t, j):
        _row_copy(h_ref, t, xs_ref, dest_ref[0, 0, 2 * t], sem).start(priority=0)
        _row_copy(h_ref, t, xs_ref, dest_ref[0, 0, 2 * t + 1], sem).start(priority=1)

    _for_rows(td, issue)
    whole = pltpu.make_async_copy(h_ref, xs_ref.at[pl.ds(0, td)], sem)
    whole.wait()
    whole.wait()


def _dispatch(dest, h2, xs_init, td):
    n, d = h2.shape
    return pl.pallas_call(
        _dispatch_kernel,
        grid=(n // td,),
        in_specs=[
            pl.BlockSpec((1, 1, 2 * td), lambda i: (i, 0, 0), memory_space=pltpu.SMEM),
            pl.BlockSpec((td, d), lambda i: (i, 0)),
            pl.BlockSpec(memory_space=pl.ANY),
        ],
        out_specs=pl.BlockSpec(memory_space=pl.ANY),
        out_shape=jax.ShapeDtypeStruct(xs_init.shape, xs_init.dtype),
        scratch_shapes=[pltpu.SemaphoreType.DMA(())],
        input_output_aliases={2: 0},
        compiler_params=_params(("arbitrary",)),
        name="dispatch",
    )(dest, h2, xs_init)


def _ffn_kernel(blk_e_ref, nused_ref, xs_ref, wg_ref, wu_ref, wd_ref, ys_ref):
    del blk_e_ref
    i = pl.program_id(0)

    @pl.when(i < nused_ref[0])
    def _():
        xb = xs_ref[...].astype(BF16)
        g = _dot(xb, wg_ref[0])
        u = _dot(xb, wu_ref[0])
        hb = (g * _sigmoid(g)) * u
        ys_ref[...] = _dot(hb.astype(BF16), wd_ref[0])

    @pl.when(i >= nused_ref[0])
    def _():
        ys_ref[...] = jnp.zeros_like(ys_ref)


def _ffn(blk_e, nused, xs, wg, wu, wd, rows):
    npad, d = xs.shape
    f = wg.shape[2]
    nb = npad // rows
    grid_spec = pltpu.PrefetchScalarGridSpec(
        num_scalar_prefetch=2,
        grid=(nb,),
        in_specs=[
            pl.BlockSpec((rows, d), lambda i, be, nu: (i, 0)),
            pl.BlockSpec((1, d, f), lambda i, be, nu: (be[i], 0, 0)),
            pl.BlockSpec((1, d, f), lambda i, be, nu: (be[i], 0, 0)),
            pl.BlockSpec((1, f, d), lambda i, be, nu: (be[i], 0, 0)),
        ],
        out_specs=pl.BlockSpec((rows, d), lambda i, be, nu: (i, 0)),
    )
    return pl.pallas_call(
        _ffn_kernel,
        grid_spec=grid_spec,
        out_shape=jax.ShapeDtypeStruct((npad, d), F32),
        compiler_params=_params(("arbitrary",)),
        name="ffn",
    )(blk_e, nused, xs, wg, wu, wd)


def _combine_kernel(dest_ref, x1_ref, route_ref, nfin_ref, ys_ref, out_ref, g1_ref, g2_ref, sem):
    td = x1_ref.shape[0]

    def issue(t, j):
        _row_copy(ys_ref, dest_ref[0, 0, 2 * t], g1_ref, t, sem).start(priority=0)
        _row_copy(ys_ref, dest_ref[0, 0, 2 * t + 1], g2_ref, t, sem).start(priority=1)

    _for_rows(td, issue)
    pltpu.make_async_copy(ys_ref.at[pl.ds(0, td)], g1_ref, sem).wait()
    pltpu.make_async_copy(ys_ref.at[pl.ds(0, td)], g2_ref, sem).wait()

    r = route_ref[...]
    y = x1_ref[...] + (r[:, 2:3] * g1_ref[...] + r[:, 3:4] * g2_ref[...])
    out_ref[...] = y * lax.rsqrt(jnp.mean(y * y, axis=-1, keepdims=True) + RMS_EPS) * nfin_ref[...]


def _combine(dest, x1, route, nfin, ys, td):
    n, d = x1.shape
    return pl.pallas_call(
        _combine_kernel,
        grid=(n // td,),
        in_specs=[
            pl.BlockSpec((1, 1, 2 * td), lambda i: (i, 0, 0), memory_space=pltpu.SMEM),
            pl.BlockSpec((td, d), lambda i: (i, 0)),
            pl.BlockSpec((td, LANES), lambda i: (i, 0)),
            pl.BlockSpec((1, d), lambda i: (0, 0)),
            pl.BlockSpec(memory_space=pl.ANY),
        ],
        out_specs=pl.BlockSpec((td, d), lambda i: (i, 0)),
        out_shape=jax.ShapeDtypeStruct((n, d), F32),
        scratch_shapes=[pltpu.VMEM((td, d), F32), pltpu.VMEM((td, d), F32), pltpu.SemaphoreType.DMA(())],
        compiler_params=_params(("arbitrary",)),
        name="combine",
    )(dest, x1, route, nfin, ys)


def _moe(x1, h2, route, wg, wu, wd, nfin, tril, td):
    n, d = x1.shape
    rank, cnt = _rank(route, tril)
    rows = MOE_ROWS if 2 * n >= N_EXPERTS * MOE_ROWS else MOE_ROWS_SMALL
    counts = cnt[0, :N_EXPERTS].astype(I32)
    pcounts = (counts + rows - 1) // rows * rows
    pends = jnp.cumsum(pcounts)
    pstarts = pends - pcounts
    nb = (2 * n) // rows + N_EXPERTS
    blk_e = jnp.minimum(
        jnp.sum(pends[None, :] <= (jnp.arange(nb, dtype=I32) * rows)[:, None], axis=1), N_EXPERTS - 1
    ).astype(I32)
    nused = (pends[-1:] // rows).astype(I32)
    pstart_row = jnp.pad(pstarts.astype(F32), (0, LANES - N_EXPERTS))[None, :]
    dest = _dest(route, rank, pstart_row, min(n, 1024))[:, 0:2].reshape(n // td, 1, 2 * td)
    xs = _dispatch(dest, h2, jnp.zeros((nb * rows, d), F32), td)
    ys = _ffn(blk_e, nused, xs, wg, wu, wd, rows)
    return _combine(dest, x1, route, nfin, ys, td)


def _layer(x, past_k, past_v, s0, conv_buf, wts, consts, *, prompt):
    (norm_mix, w_in_parts, bias, conv_w, alog, dtb, gnorm, wa, wb, wo, norm_ffn, wrh, wrl,
     wg, wu, wd, norm_final) = wts
    batch, seq, d = x.shape
    n = batch * seq
    left = LEFT_CHUNKS * CHUNK
    tm = 512 if n % 512 == 0 else n
    xf = x.reshape(n, d)
    keep = min(left, seq)
    assert (seq % tm == 0 and tm == keep) or (tm % seq == 0 and keep == seq)
    tiles_per_seq = max(seq // tm, 1)
    qkv, kv, xbc, gb, ab, gate = _inproj(xf, norm_mix, w_in_parts, tm, tiles_per_seq)

    if prompt:
        o_a = _attention(qkv, qkv, qkv, bias, batch=batch, seq=seq, cpb=LEFT_CHUNKS, prev_is_cache=False)
        rows = left
        s_init = jnp.zeros((batch, N_PAIRS, HEAD_DIM, LANES), F32)
        tail = jnp.zeros((batch, CONV_TAIL, 3 * WIDTH), F32)
    else:
        cache = jnp.concatenate(
            [past_k.reshape(batch * left, WIDTH), past_v.reshape(batch * left, WIDTH)], axis=1).astype(BF16)
        o_a = _attention(qkv, cache, qkv, bias, batch=batch, seq=seq, cpb=seq // CHUNK, prev_is_cache=True)
        rows = seq
        s_init = s0.reshape(batch, N_PAIRS, 2, HEAD_DIM, HEAD_DIM).transpose(0, 1, 3, 2, 4).reshape(
            batch, N_PAIRS, HEAD_DIM, LANES)
        tail = jnp.pad(conv_buf, ((0, 0), (CONV_TAIL - (CONV_W - 1), 0), (0, 0)))

    bd, eb, eg, lt, eye2, tril = consts
    o_b, s_new = _gdn(xbc, gb, ab, conv_w, alog, dtb, gnorm, s_init, tail,
                      (bd, eb, eg, lt, eye2), batch=batch, seq=seq, rows=rows)

    x1, h2, route = _post(o_a, o_b, gate, xf, wa, wb, wo, norm_ffn, wrh, wrl, tm)
    y = _moe(x1, h2, route, wg, wu, wd, norm_final, tril, td=min(512, n))

    kv4 = kv.reshape(batch, keep, 2, N_HEADS, HEAD_DIM)
    new_k = kv4[:, :, 0]
    new_v = kv4[:, :, 1]
    new_s = s_new.reshape(batch, N_PAIRS, HEAD_DIM, 2, HEAD_DIM).transpose(0, 1, 3, 2, 4).reshape(
        batch, N_HEADS, HEAD_DIM, HEAD_DIM)
    new_conv = xbc.reshape(batch, seq, 3 * WIDTH)[:, seq - (CONV_W - 1):]
    return y.reshape(batch, seq, d), new_k, new_v, new_s, new_conv


def _constants():
    lane = jnp.arange(WIDTH)
    src = jnp.arange(LANES)
    bd = (src[:, None] // HEAD_DIM == src[None, :] // HEAD_DIM).astype(BF16)
    eb = (src[:, None] == lane[None, :] // HEAD_DIM).astype(BF16)
    eg = (src[:, None] == N_HEADS + lane[None, :] // HEAD_DIM).astype(BF16)
    c = jnp.arange(CHUNK)
    lt = (c[:, None] >= c[None, :]).astype(BF16)
    eye2 = (c[:, None] == jnp.arange(LANES)[None, :] % HEAD_DIM).astype(F32)
    t = jnp.arange(512)
    tril = (t[:, None] > t[None, :]).astype(BF16)
    return bd, eb, eg, lt, eye2, tril


def kernel(x_prompt, x_sample, cache_attn_k, cache_attn_v, state_delta, state_conv, norm_mix, w_in, rel_bias,
           conv_w, a_log, dt_bias, gdn_norm, w_branch_a, w_branch_b, w_out, norm_ffn, w_route_group,
           w_route_expert, w_gate, w_up, w_down, norm_final):
    depth = w_in.shape[0]
    d = x_prompt.shape[-1]
    left = LEFT_CHUNKS * CHUNK
    consts = _constants()
    qi = jnp.arange(CHUNK)
    kj = jnp.arange(BAND)
    dist = jnp.clip(qi[:, None] - kj[None, :] + left, -REL_CLIP, REL_CLIP) + REL_CLIP

    xp, xs = x_prompt, x_sample
    outs_p, outs_s = [], []
    for l in range(depth):
        w = w_in[l]
        o = 0
        parts = []
        for width in (3 * WIDTH, 3 * WIDTH, WIDTH, 2 * N_HEADS, 2 * d):
            parts.append(w[:, o:o + width])
            o += width
        wqkv, wxbc, wgb, wab, wgate = parts
        wqkv = jnp.concatenate([wqkv[:, :WIDTH] * (HEAD_DIM ** -0.5), wqkv[:, WIDTH:]], axis=1)
        wab = jnp.pad(wab, ((0, 0), (0, LANES - 2 * N_HEADS)))
        w_in_parts = tuple(m.astype(BF16) for m in (wqkv, wxbc, wgb, wab, wgate))
        lane_pad = lambda v: jnp.pad(v[None, :], ((0, 0), (N_HEADS, LANES - 2 * N_HEADS)))
        wr = jnp.pad(jnp.concatenate([w_route_group[l], w_route_expert[l]], axis=1),
                     ((0, 0), (0, LANES - N_GROUPS - N_EXPERTS)))
        wrh = wr.astype(BF16)
        wrl = (wr - wrh.astype(F32)).astype(BF16)
        wts = (
            norm_mix[l][None, :], w_in_parts, rel_bias[l][:, dist], conv_w[l], lane_pad(a_log[l]),
            lane_pad(dt_bias[l]), jnp.tile(gdn_norm[l], N_HEADS)[None, :],
            w_branch_a[l].astype(BF16), w_branch_b[l].astype(BF16), w_out[l].astype(BF16),
            norm_ffn[l][None, :], wrh, wrl,
            w_gate[l].astype(BF16), w_up[l].astype(BF16), w_down[l].astype(BF16), norm_final[None, :],
        )
        assert depth == 1
        xp, kp, vp, sp, cp = _layer(xp, None, None, None, None, wts, consts, prompt=True)
        xs, ks, vs, ss, cs = _layer(xs, cache_attn_k[l], cache_attn_v[l], state_delta[l], state_conv[l],
                                    wts, consts, prompt=False)
        outs_p.append((kp, vp, sp, cp))
        outs_s.append((ks, vs, ss, cs))
    stack = lambda items, idx: jnp.stack([it[idx] for it in items])
    return (xp, xs,
            stack(outs_p, 0), stack(outs_p, 1), stack(outs_p, 2), stack(outs_p, 3),
            stack(outs_s, 0), stack(outs_s, 1), stack(outs_s, 2), stack(outs_s, 3))
```

```python
import functools

import jax
import jax.numpy as jnp
from jax import lax
from jax.experimental import pallas as pl
from jax.experimental.pallas import tpu as pltpu

F32 = jnp.float32
BF16 = jnp.bfloat16
I32 = jnp.int32

RMS_EPS = 1e-6
CHUNK = 64
LEFT_CHUNKS = 8
BAND = (LEFT_CHUNKS + 1) * CHUNK
N_HEADS = 8
HEAD_DIM = 64
N_PAIRS = N_HEADS // 2
WIDTH = N_HEADS * HEAD_DIM
REL_CLIP = 128
CONV_W = 4
N_GROUPS = 4
EXPERTS_PER_GROUP = 8
N_EXPERTS = N_GROUPS * EXPERTS_PER_GROUP
LANES = 128
CONV_TAIL = 8
MOE_ROWS = 512
MOE_ROWS_SMALL = 128
ROW_UNROLL = 8
VMEM_LIMIT = 56 * 1024 * 1024


def _dot(a, b):
    return jnp.dot(a, b, preferred_element_type=F32)


def _dot_nt(a, b):
    return lax.dot_general(a, b, (((1,), (1,)), ((), ())), preferred_element_type=F32)


def _dot_tn(a, b):
    return lax.dot_general(a, b, (((0,), (0,)), ((), ())), preferred_element_type=F32)


def _sigmoid(x):
    return 1.0 / (1.0 + jnp.exp(-x))


def _split2(x):
    hi = x.astype(BF16)
    lo = (x - hi.astype(F32)).astype(BF16)
    return hi, lo


def _split3(x):
    hi = x.astype(BF16)
    r = x - hi.astype(F32)
    mid = r.astype(BF16)
    lo = (r - mid.astype(F32)).astype(BF16)
    return hi, mid, lo


def _dot3_rhs01(x, mat):
    hi, mid, lo = _split3(x)
    return (_dot(hi, mat) + _dot(mid, mat)) + _dot(lo, mat)


def _dot3_lhs01(mat, x):
    hi, mid, lo = _split3(x)
    return (_dot(mat, hi) + _dot(mat, mid)) + _dot(mat, lo)


def _group_sum(x2, bd):
    hi, lo = _split2(x2)
    outs = []
    for p in range(x2.shape[1] // LANES):
        cols = slice(p * LANES, (p + 1) * LANES)
        outs.append(_dot(hi[:, cols], bd) + _dot(lo[:, cols], bd))
    return jnp.concatenate(outs, axis=1)


def _params(sem):
    return pltpu.CompilerParams(dimension_semantics=sem, vmem_limit_bytes=VMEM_LIMIT)


def _conv_silu(x, w, tail_scr, xc_scr):
    rows = x.shape[0]
    xc_scr[0:CONV_TAIL, :] = tail_scr[...]
    xc_scr[CONV_TAIL:, :] = x
    y = xc_scr[CONV_TAIL - 3:CONV_TAIL - 3 + rows, :] * w[0:1]
    y = y + xc_scr[CONV_TAIL - 2:CONV_TAIL - 2 + rows, :] * w[1:2]
    y = y + xc_scr[CONV_TAIL - 1:CONV_TAIL - 1 + rows, :] * w[2:3]
    y = y + x * w[3:4]
    tail_scr[...] = x[rows - CONV_TAIL:rows]
    return y * _sigmoid(y)


def _inproj_kernel(x_ref, g_ref, wqkv_ref, wxbc_ref, wgb_ref, wab_ref, wgate_ref,
                   qkv_ref, kv_ref, xbc_ref, gb_ref, ab_ref, gate_ref, *, tiles_per_seq):
    x = x_ref[...]
    h = x * lax.rsqrt(jnp.mean(x * x, axis=-1, keepdims=True) + RMS_EPS) * g_ref[...]
    hb = h.astype(BF16)
    qkv_ref[...] = _dot(hb, wqkv_ref[...]).astype(BF16)

    @pl.when(pl.program_id(0) % tiles_per_seq == tiles_per_seq - 1)
    def _():
        kv_ref[...] = _dot(hb, wqkv_ref[:, WIDTH:])

    xbc_ref[...] = _dot(hb, wxbc_ref[...])
    gb_ref[...] = _dot(hb, wgb_ref[...])
    ab_ref[...] = _dot(hb, wab_ref[...])
    gate_ref[...] = _dot(hb, wgate_ref[...])


def _inproj(x, g, w, tm, tiles_per_seq):
    n, d = x.shape
    wqkv, wxbc, wgb, wab, wgate = w
    row = lambda i: (i, 0)
    tail = lambda i: (i // tiles_per_seq, 0)
    const = lambda i: (0, 0)
    widths = (3 * WIDTH, 2 * WIDTH, 3 * WIDTH, WIDTH, LANES, 2 * d)
    dtypes = (BF16, F32, F32, F32, F32, F32)
    nrows = (n, n // tiles_per_seq, n, n, n, n)
    maps = (row, tail, row, row, row, row)
    return pl.pallas_call(
        functools.partial(_inproj_kernel, tiles_per_seq=tiles_per_seq),
        grid=(n // tm,),
        in_specs=[pl.BlockSpec((tm, d), row), pl.BlockSpec((1, d), const)]
        + [pl.BlockSpec(m.shape, const) for m in w],
        out_specs=[pl.BlockSpec((tm, c), m) for c, m in zip(widths, maps)],
        out_shape=[jax.ShapeDtypeStruct((r, c), t) for r, c, t in zip(nrows, widths, dtypes)],
        compiler_params=_params(("arbitrary",)),
        name="inproj",
    )(x, g, wqkv, wxbc, wgb, wab, wgate)


def _attn_kernel(q_ref, kp_ref, kc_ref, vp_ref, vc_ref, bias_ref, o_ref, kw_ref, vw_ref, s_ref, p_ref,
                 *, cpb, mask_first):
    left = LEFT_CHUNKS * CHUNK
    kw_ref[0:left, :] = kp_ref[...]
    kw_ref[left:, :] = kc_ref[...]
    vw_ref[0:left, :] = vp_ref[...]
    vw_ref[left:, :] = vc_ref[...]
    lane = lax.broadcasted_iota(I32, (CHUNK, LANES), 1)
    m0 = lane < HEAD_DIM
    first_block = pl.program_id(1) == 0

    def chunk(i, carry):
        r0 = pl.multiple_of(i * CHUNK, CHUNK)
        if mask_first:
            col = lax.broadcasted_iota(I32, (CHUNK, BAND), 1) + r0
            valid = jnp.logical_or(jnp.logical_not(first_block), col >= left)
        for p in range(N_PAIRS):
            cols = slice(p * LANES, (p + 1) * LANES)
            q = q_ref[pl.ds(r0, CHUNK), cols]
            k = kw_ref[pl.ds(r0, BAND), cols]
            for hh in range(2):
                qm = jnp.where(m0 if hh == 0 else jnp.logical_not(m0), q, jnp.zeros_like(q))
                s = _dot_nt(qm, k) + bias_ref[2 * p + hh]
                if mask_first:
                    s = jnp.where(valid, s, -1e30)
                s_ref[2 * p + hh] = s
        for h in range(N_HEADS):
            s = s_ref[h]
            e = jnp.exp(s - jnp.max(s, axis=-1, keepdims=True))
            p_ref[h] = (e / jnp.sum(e, axis=-1, keepdims=True)).astype(BF16)
        pairs = []
        for p in range(N_PAIRS):
            v = vw_ref[pl.ds(r0, BAND), p * LANES:(p + 1) * LANES]
            pairs.append(jnp.where(m0, _dot(p_ref[2 * p], v), _dot(p_ref[2 * p + 1], v)))
        o_ref[pl.ds(r0, CHUNK), :] = jnp.concatenate(pairs, axis=1).astype(o_ref.dtype)
        return carry

    lax.fori_loop(0, cpb, chunk, 0)


def _attention(q_src, k_prev_src, kv_cur_src, bias, *, batch, seq, cpb, prev_is_cache):
    left = LEFT_CHUNKS * CHUNK
    rows = cpb * CHUNK
    nblk = seq // rows
    if prev_is_cache:
        prev_idx = lambda col: (lambda b, j: (b, col))
        kcol, vcol = 0, 1
    else:
        assert rows == left
        prev_idx = lambda col: (lambda b, j: (b * nblk + jnp.maximum(j - 1, 0), col))
        kcol, vcol = 1, 2
    cur = lambda col: (lambda b, j: (b * nblk + j, col))
    kern = functools.partial(_attn_kernel, cpb=cpb, mask_first=not prev_is_cache)
    return pl.pallas_call(
        kern,
        grid=(batch, nblk),
        in_specs=[
            pl.BlockSpec((rows, WIDTH), cur(0)),
            pl.BlockSpec((left, WIDTH), prev_idx(kcol)),
            pl.BlockSpec((rows, WIDTH), cur(1)),
            pl.BlockSpec((left, WIDTH), prev_idx(vcol)),
            pl.BlockSpec((rows, WIDTH), cur(2)),
            pl.BlockSpec(bias.shape, lambda b, j: (0, 0, 0)),
        ],
        out_specs=pl.BlockSpec((rows, WIDTH), lambda b, j: (b * nblk + j, 0)),
        out_shape=jax.ShapeDtypeStruct((batch * seq, WIDTH), BF16),
        scratch_shapes=[pltpu.VMEM((left + rows, WIDTH), BF16), pltpu.VMEM((left + rows, WIDTH), BF16),
                        pltpu.VMEM((N_HEADS, CHUNK, BAND), F32), pltpu.VMEM((N_HEADS, CHUNK, BAND), BF16)],
        compiler_params=_params(("parallel", "arbitrary")),
        name="attn",
    )(q_src, k_prev_src, kv_cur_src, k_prev_src, kv_cur_src, bias)


def _blockdiag(b, m0):
    z = jnp.zeros_like(b)
    return jnp.concatenate([jnp.where(m0, b, z), jnp.where(m0, z, b)], axis=0).astype(BF16)


def _pairdot(a, b, m0):
    return _dot(a.astype(BF16), _blockdiag(b, m0))


def _gdn_kernel(xbc_ref, gb_ref, ab_ref, convw_ref, alog_ref, dtb_ref, gnorm_ref, s0_ref, tail0_ref,
                bd_ref, eb_ref, eg_ref, lt_ref, eye2_ref,
                ob_ref, sout_ref,
                s_scr, tail_scr, xc_scr, kdt_scr, u_scr, q_scr, k_scr, v_scr, beta_scr, gc_scr, o_scr, pw_scr, t_scr, aqk_scr,
                *, rows):
    nchunks = rows // CHUNK

    ri = lax.broadcasted_iota(I32, (CHUNK, LANES), 0)
    ci = lax.broadcasted_iota(I32, (CHUNK, LANES), 1)
    m0 = ci < HEAD_DIM

    @pl.when(pl.program_id(1) == 0)
    def _():
        for p in range(N_PAIRS):
            s = s0_ref[0, p]
            z = jnp.zeros_like(s)
            s_scr[p] = jnp.concatenate([jnp.where(m0, s, z), jnp.where(m0, z, s)], axis=0)
        tail_scr[...] = tail0_ref[0]

    y = _conv_silu(xbc_ref[...], convw_ref[...], tail_scr, xc_scr)

    bd = bd_ref[...]
    q = y[:, 0:WIDTH]
    k = y[:, WIDTH:2 * WIDTH]
    q_scr[...] = q * lax.rsqrt(_group_sum(q * q, bd) + RMS_EPS) * (HEAD_DIM ** -0.5)
    k_scr[...] = k * lax.rsqrt(_group_sum(k * k, bd) + RMS_EPS)
    v_scr[...] = y[:, 2 * WIDTH:3 * WIDTH]

    ab = ab_ref[...]
    z = ab + dtb_ref[...]
    softplus = jnp.maximum(z, 0.0) + jnp.log1p(jnp.exp(-jnp.abs(z)))
    g = -jnp.exp(alog_ref[...]) * softplus
    beta_scr[...] = _dot3_rhs01(_sigmoid(ab), eb_ref[...])
    gexp = _dot3_rhs01(g, eg_ref[...])
    lt = lt_ref[...]
    for c in range(nchunks):
        gc_scr[c * CHUNK:(c + 1) * CHUNK, :] = _dot3_lhs01(lt, gexp[c * CHUNK:(c + 1) * CHUNK])

    cj = jnp.bitwise_and(ci, HEAD_DIM - 1)
    incl = ri >= cj
    strict = ri > cj
    eye2 = eye2_ref[...]
    ones = jnp.ones((CHUNK, CHUNK), BF16)
    r2 = lax.broadcasted_iota(I32, (LANES, LANES), 0)
    c2 = lax.broadcasted_iota(I32, (LANES, LANES), 1)
    eye128 = jnp.where(r2 == c2, 1.0, 0.0).astype(BF16)
    same_head = (r2 < HEAD_DIM) == (c2 < HEAD_DIM)

    def tiles(body, unroll):
        def step(c, carry):
            if isinstance(c, int):
                rs = slice(c * CHUNK, (c + 1) * CHUNK)
            else:
                rs = pl.ds(pl.multiple_of(c * CHUNK, CHUNK), CHUNK)
            for p in range(N_PAIRS):
                body(rs, slice(p * LANES, (p + 1) * LANES), c * N_PAIRS + p, p)
            return carry
        if nchunks <= unroll:
            for c in range(nchunks):
                step(c, 0)
        else:
            lax.fori_loop(0, nchunks, step, 0, unroll=unroll)

    def setup(rs, cols, tile, p):
        qp = q_scr[rs, cols]
        kp = k_scr[rs, cols]
        beta = beta_scr[rs, cols]
        gc = gc_scr[rs, cols]
        gl = gc[CHUNK - 1:CHUNK, :]
        eg = jnp.exp(gc)
        kb = kp * beta
        bdk = _blockdiag(kp, m0)
        kk = _dot_nt(kb.astype(BF16), bdk)
        qk = _dot_nt(qp.astype(BF16), bdk)
        gcol = _dot3_lhs01(ones, gc * eye2)
        decay = jnp.where(incl, jnp.exp(jnp.where(incl, gc - gcol, 0.0)), 0.0)
        nmat = -(kk * jnp.where(strict, decay, 0.0))
        pw_scr[rs, cols] = nmat
        t_scr[rs, cols] = eye2 + nmat
        aqk_scr[rs, cols] = qk * decay
        q_scr[rs, cols] = qp * eg
        kd = (kp * jnp.exp(gl - gc)).astype(BF16)
        kdt_scr[tile] = _dot_nt(eye128, kd).astype(BF16)
        v_scr[rs, cols] = v_scr[rs, cols] * beta
        beta_scr[rs, cols] = kb * eg

    tiles(setup, 8)

    def double(rs, cols, tile, p):
        pw = pw_scr[rs, cols]
        pw = _pairdot(pw, pw, m0)
        pw_scr[rs, cols] = pw
        t = t_scr[rs, cols]
        t_scr[rs, cols] = t + _pairdot(t, pw, m0)

    for _ in range(5):
        tiles(double, 8)

    def solve(rs, cols, tile, p):
        t = t_scr[rs, cols]
        v_scr[rs, cols] = _pairdot(t, v_scr[rs, cols], m0)
        beta_scr[rs, cols] = _pairdot(t, beta_scr[rs, cols], m0)

    tiles(solve, 8)

    def scan_u(rs, cols, tile, p):
        u_scr[p] = v_scr[rs, cols] - _dot(beta_scr[rs, cols].astype(BF16), s_scr[p].astype(BF16))

    def scan_s(rs, cols, tile, p):
        s = s_scr[p]
        u = u_scr[p]
        o_scr[rs, cols] = (_dot(q_scr[rs, cols].astype(BF16), s.astype(BF16))
                           + _pairdot(aqk_scr[rs, cols], u, m0))
        f = _dot(kdt_scr[tile], u.astype(BF16))
        gt = jnp.exp(gc_scr[rs, cols][CHUNK - 1:CHUNK, :])
        s_scr[p] = s * gt + jnp.where(same_head, f, 0.0)

    def scan(c, carry):
        rs = pl.ds(pl.multiple_of(c * CHUNK, CHUNK), CHUNK)
        for body in (scan_u, scan_s):
            for p in range(N_PAIRS):
                body(rs, slice(p * LANES, (p + 1) * LANES), c * N_PAIRS + p, p)
        return carry

    lax.fori_loop(0, nchunks, scan, 0)

    o = o_scr[...]
    on = o * lax.rsqrt(_group_sum(o * o, bd) * (1.0 / HEAD_DIM) + RMS_EPS) * gnorm_ref[...]
    gate = gb_ref[...]
    ob_ref[...] = (on * (gate * _sigmoid(gate))).astype(ob_ref.dtype)
    for p in range(N_PAIRS):
        s = s_scr[p]
        sout_ref[0, p] = jnp.where(m0, s[0:HEAD_DIM], s[HEAD_DIM:2 * HEAD_DIM])


def _gdn(xbc, gb, ab, convw, alog, dtb, gnorm, s0, tail0, consts, *, batch, seq, rows):
    nblk = seq // rows
    bd, eb, eg, lt, eye2 = consts
    blk = lambda b, j: (b * nblk + j, 0)
    c2 = lambda b, j: (0, 0)
    per_b3 = lambda b, j: (b, 0, 0)
    per_b4 = lambda b, j: (b, 0, 0, 0)
    kern = functools.partial(_gdn_kernel, rows=rows)
    wide = pltpu.VMEM((rows, WIDTH), F32)
    return pl.pallas_call(
        kern,
        grid=(batch, nblk),
        in_specs=[
            pl.BlockSpec((rows, 3 * WIDTH), blk),
            pl.BlockSpec((rows, WIDTH), blk),
            pl.BlockSpec((rows, LANES), blk),
            pl.BlockSpec(convw.shape, c2),
            pl.BlockSpec(alog.shape, c2),
            pl.BlockSpec(dtb.shape, c2),
            pl.BlockSpec(gnorm.shape, c2),
            pl.BlockSpec((1, N_PAIRS, HEAD_DIM, LANES), per_b4),
            pl.BlockSpec((1, CONV_TAIL, 3 * WIDTH), per_b3),
            pl.BlockSpec(bd.shape, c2),
            pl.BlockSpec(eb.shape, c2),
            pl.BlockSpec(eg.shape, c2),
            pl.BlockSpec(lt.shape, c2),
            pl.BlockSpec(eye2.shape, c2),
        ],
        out_specs=[
            pl.BlockSpec((rows, WIDTH), blk),
            pl.BlockSpec((1, N_PAIRS, HEAD_DIM, LANES), per_b4),
        ],
        out_shape=[
            jax.ShapeDtypeStruct((batch * seq, WIDTH), BF16),
            jax.ShapeDtypeStruct((batch, N_PAIRS, HEAD_DIM, LANES), F32),
        ],
        scratch_shapes=[
            pltpu.VMEM((N_PAIRS, LANES, LANES), F32),
            pltpu.VMEM((CONV_TAIL, 3 * WIDTH), F32),
            pltpu.VMEM((CONV_TAIL + rows, 3 * WIDTH), F32),
            pltpu.VMEM((rows // CHUNK * N_PAIRS, LANES, HEAD_DIM), BF16),
            pltpu.VMEM((N_PAIRS, CHUNK, LANES), F32),
            wide, wide, wide, wide, wide, wide, wide, wide, wide,
        ],
        compiler_params=_params(("parallel", "arbitrary")),
        name="gdn",
    )(xbc, gb, ab, convw, alog, dtb, gnorm, s0, tail0, bd, eb, eg, lt, eye2)


def _post_kernel(oa_ref, ob_ref, gate_ref, x_ref, wa_ref, wb_ref, wo_ref, nf_ref, wrh_ref, wrl_ref,
                 x1_ref, h2_ref, route_ref):
    d = x_ref.shape[1]
    a = _dot(oa_ref[...], wa_ref[...])
    b = _dot(ob_ref[...], wb_ref[...])
    merged = _sigmoid(gate_ref[:, 0:d]) * a + _sigmoid(gate_ref[:, d:2 * d]) * b
    x1 = x_ref[...] + _dot(merged.astype(BF16), wo_ref[...])
    x1_ref[...] = x1
    h2 = x1 * lax.rsqrt(jnp.mean(x1 * x1, axis=-1, keepdims=True) + RMS_EPS) * nf_ref[...]
    h2_ref[...] = h2

    hi, lo = _split2(h2)
    wrh = wrh_ref[...]
    logits = _dot(hi, wrh) + (_dot(lo, wrh) + _dot(hi, wrl_ref[...]))

    lane = lax.broadcasted_iota(I32, logits.shape, 1).astype(F32)
    big = float(LANES)
    ninf = -jnp.inf
    lg = jnp.where(lane < N_GROUPS, logits, ninf)
    mg = jnp.max(lg, axis=-1, keepdims=True)
    grp = jnp.min(jnp.where(lg == mg, lane, big), axis=-1, keepdims=True)
    p_grp = 1.0 / jnp.sum(jnp.exp(lg - mg), axis=-1, keepdims=True)
    lo_lane = N_GROUPS + grp * EXPERTS_PER_GROUP
    le = jnp.where(jnp.logical_and(lane >= lo_lane, lane < lo_lane + EXPERTS_PER_GROUP), logits, ninf)
    m1 = jnp.max(le, axis=-1, keepdims=True)
    i1 = jnp.min(jnp.where(le == m1, lane, big), axis=-1, keepdims=True)
    le2 = jnp.where(lane == i1, ninf, le)
    m2 = jnp.max(le2, axis=-1, keepdims=True)
    i2 = jnp.min(jnp.where(le2 == m2, lane, big), axis=-1, keepdims=True)
    e2 = jnp.exp(m2 - m1)
    w1 = 1.0 / (1.0 + e2) * p_grp
    w2 = e2 / (1.0 + e2) * p_grp
    out = jnp.where(lane == 0, i1 - N_GROUPS, 0.0)
    out = jnp.where(lane == 1, i2 - N_GROUPS, out)
    out = jnp.where(lane == 2, w1, out)
    out = jnp.where(lane == 3, w2, out)
    route_ref[...] = out


def _post(oa, ob, gate, x, wa, wb, wo, nf, wrh, wrl, tm):
    n, d = x.shape
    row = lambda i: (i, 0)
    const = lambda i: (0, 0)
    return pl.pallas_call(
        _post_kernel,
        grid=(n // tm,),
        in_specs=[
            pl.BlockSpec((tm, WIDTH), row), pl.BlockSpec((tm, WIDTH), row),
            pl.BlockSpec((tm, 2 * d), row), pl.BlockSpec((tm, d), row),
            pl.BlockSpec(wa.shape, const), pl.BlockSpec(wb.shape, const), pl.BlockSpec(wo.shape, const),
            pl.BlockSpec(nf.shape, const), pl.BlockSpec(wrh.shape, const), pl.BlockSpec(wrl.shape, const),
        ],
        out_specs=[pl.BlockSpec((tm, d), row), pl.BlockSpec((tm, d), row), pl.BlockSpec((tm, LANES), row)],
        out_shape=[
            jax.ShapeDtypeStruct((n, d), F32),
            jax.ShapeDtypeStruct((n, d), F32),
            jax.ShapeDtypeStruct((n, LANES), F32),
        ],
        compiler_params=_params(("parallel",)),
        name="post",
    )(oa, ob, gate, x, wa, wb, wo, nf, wrh, wrl)


def _rank_kernel(route_ref, tril_ref, rank_ref, cnt_ref, carry_scr):
    @pl.when(pl.program_id(0) == 0)
    def _():
        carry_scr[...] = jnp.zeros_like(carry_scr)

    r = route_ref[...]
    lane = lax.broadcasted_iota(I32, r.shape, 1)
    lanef = lane.astype(F32)
    oh1 = lanef == r[:, 0:1]
    oh2 = lanef == r[:, 1:2]
    oh = jnp.where(jnp.logical_or(oh1, oh2), 1.0, 0.0)
    before = _dot(tril_ref[...], oh.astype(BF16)) + carry_scr[...]
    rank1 = jnp.sum(jnp.where(oh1, before, 0.0), axis=-1, keepdims=True)
    rank2 = jnp.sum(jnp.where(oh2, before, 0.0), axis=-1, keepdims=True)
    rank_ref[...] = jnp.where(lane == 0, rank1, jnp.where(lane == 1, rank2, 0.0))
    carry_scr[...] = carry_scr[...] + jnp.sum(oh, axis=0, keepdims=True)
    cnt_ref[...] = carry_scr[...]


def _rank(route, tril):
    n = route.shape[0]
    tr = min(tril.shape[0], n)
    tril = tril[:tr, :tr]
    return pl.pallas_call(
        _rank_kernel,
        grid=(n // tr,),
        in_specs=[pl.BlockSpec((tr, LANES), lambda i: (i, 0)), pl.BlockSpec(tril.shape, lambda i: (0, 0))],
        out_specs=[pl.BlockSpec((tr, LANES), lambda i: (i, 0)), pl.BlockSpec((1, LANES), lambda i: (0, 0))],
        out_shape=[jax.ShapeDtypeStruct((n, LANES), F32), jax.ShapeDtypeStruct((1, LANES), F32)],
        scratch_shapes=[pltpu.VMEM((1, LANES), F32)],
        compiler_params=_params(("arbitrary",)),
        name="rank",
    )(route, tril)


def _dest_kernel(route_ref, rank_ref, pstart_ref, dest_ref):
    r = route_ref[...]
    rk = rank_ref[...]
    ps = pstart_ref[...]
    lane = lax.broadcasted_iota(I32, r.shape, 1)
    lanef = lane.astype(F32)
    d1 = jnp.sum(jnp.where(lanef == r[:, 0:1], ps, 0.0), axis=-1, keepdims=True) + rk[:, 0:1]
    d2 = jnp.sum(jnp.where(lanef == r[:, 1:2], ps, 0.0), axis=-1, keepdims=True) + rk[:, 1:2]
    dest_ref[...] = jnp.where(lane == 0, d1, jnp.where(lane == 1, d2, 0.0)).astype(I32)


def _dest(route, rank, pstart, tr):
    n = route.shape[0]
    row = lambda i: (i, 0)
    return pl.pallas_call(
        _dest_kernel,
        grid=(n // tr,),
        in_specs=[pl.BlockSpec((tr, LANES), row), pl.BlockSpec((tr, LANES), row),
                  pl.BlockSpec((1, LANES), lambda i: (0, 0))],
        out_specs=pl.BlockSpec((tr, LANES), row),
        out_shape=jax.ShapeDtypeStruct((n, LANES), I32),
        compiler_params=_params(("parallel",)),
        name="dest",
    )(route, rank, pstart)


def _row_copy(src_ref, src_row, dst_ref, dst_row, sem):
    return pltpu.make_async_copy(src_ref.at[pl.ds(src_row, 1)], dst_ref.at[pl.ds(dst_row, 1)], sem)


def _for_rows(td, per_row):
    def group(i, carry):
        for j in range(ROW_UNROLL):
            per_row(i * ROW_UNROLL + j, j)
        return carry
    lax.fori_loop(0, td // ROW_UNROLL, group, 0)


def _dispatch_kernel(dest_ref, h_ref, xs_in_ref, xs_ref, sem):
    del xs_in_ref
    td = h_ref.shape[0]

    def issue(t, j):
        _row_copy(h_ref, t, xs_ref, dest_ref[0, 0, 2 * t], sem).start(priority=0)
        _row_copy(h_ref, t, xs_ref, dest_ref[0, 0, 2 * t + 1], sem).start(priority=1)

    _for_rows(td, issue)
    whole = pltpu.make_async_copy(h_ref, xs_ref.at[pl.ds(0, td)], sem)
    whole.wait()
    whole.wait()


def _dispatch(dest, h2, xs_init, td):
    n, d = h2.shape
    return pl.pallas_call(
        _dispatch_kernel,
        grid=(n // td,),
        in_specs=[
            pl.BlockSpec((1, 1, 2 * td), lambda i: (i, 0, 0), memory_space=pltpu.SMEM),
            pl.BlockSpec((td, d), lambda i: (i, 0)),
            pl.BlockSpec(memory_space=pl.ANY),
        ],
        out_specs=pl.BlockSpec(memory_space=pl.ANY),
        out_shape=jax.ShapeDtypeStruct(xs_init.shape, xs_init.dtype),
        scratch_shapes=[pltpu.SemaphoreType.DMA(())],
        input_output_aliases={2: 0},
        compiler_params=_params(("arbitrary",)),
        name="dispatch",
    )(dest, h2, xs_init)


def _ffn_kernel(blk_e_ref, nused_ref, xs_ref, wg_ref, wu_ref, wd_ref, ys_ref):
    del blk_e_ref
    i = pl.program_id(0)

    @pl.when(i < nused_ref[0])
    def _():
        xb = xs_ref[...].astype(BF16)
        g = _dot(xb, wg_ref[0].astype(BF16))
        u = _dot(xb, wu_ref[0].astype(BF16))
        hb = (g * _sigmoid(g)) * u
        ys_ref[...] = _dot(hb.astype(BF16), wd_ref[0].astype(BF16))

    @pl.when(i >= nused_ref[0])
    def _():
        ys_ref[...] = jnp.zeros_like(ys_ref)


def _ffn(blk_e, nused, xs, wg, wu, wd, rows):
    npad, d = xs.shape
    f = wg.shape[2]
    nb = npad // rows
    grid_spec = pltpu.PrefetchScalarGridSpec(
        num_scalar_prefetch=2,
        grid=(nb,),
        in_specs=[
            pl.BlockSpec((rows, d), lambda i, be, nu: (i, 0)),
            pl.BlockSpec((1, d, f), lambda i, be, nu: (be[i], 0, 0)),
            pl.BlockSpec((1, d, f), lambda i, be, nu: (be[i], 0, 0)),
            pl.BlockSpec((1, f, d), lambda i, be, nu: (be[i], 0, 0)),
        ],
        out_specs=pl.BlockSpec((rows, d), lambda i, be, nu: (i, 0)),
    )
    return pl.pallas_call(
        _ffn_kernel,
        grid_spec=grid_spec,
        out_shape=jax.ShapeDtypeStruct((npad, d), F32),
        compiler_params=_params(("arbitrary",)),
        name="ffn",
    )(blk_e, nused, xs, wg, wu, wd)


def _combine_kernel(dest_ref, x1_ref, route_ref, nfin_ref, ys_ref, out_ref, g1_ref, g2_ref, sem):
    td = x1_ref.shape[0]

    def issue(t, j):
        _row_copy(ys_ref, dest_ref[0, 0, 2 * t], g1_ref, t, sem).start(priority=0)
        _row_copy(ys_ref, dest_ref[0, 0, 2 * t + 1], g2_ref, t, sem).start(priority=1)

    _for_rows(td, issue)
    pltpu.make_async_copy(ys_ref.at[pl.ds(0, td)], g1_ref, sem).wait()
    pltpu.make_async_copy(ys_ref.at[pl.ds(0, td)], g2_ref, sem).wait()

    r = route_ref[...]
    y = x1_ref[...] + (r[:, 2:3] * g1_ref[...] + r[:, 3:4] * g2_ref[...])
    out_ref[...] = y * lax.rsqrt(jnp.mean(y * y, axis=-1, keepdims=True) + RMS_EPS) * nfin_ref[...]


def _combine(dest, x1, route, nfin, ys, td):
    n, d = x1.shape
    return pl.pallas_call(
        _combine_kernel,
        grid=(n // td,),
        in_specs=[
            pl.BlockSpec((1, 1, 2 * td), lambda i: (i, 0, 0), memory_space=pltpu.SMEM),
            pl.BlockSpec((td, d), lambda i: (i, 0)),
            pl.BlockSpec((td, LANES), lambda i: (i, 0)),
            pl.BlockSpec((1, d), lambda i: (0, 0)),
            pl.BlockSpec(memory_space=pl.ANY),
        ],
        out_specs=pl.BlockSpec((td, d), lambda i: (i, 0)),
        out_shape=jax.ShapeDtypeStruct((n, d), F32),
        scratch_shapes=[pltpu.VMEM((td, d), F32), pltpu.VMEM((td, d), F32), pltpu.SemaphoreType.DMA(())],
        compiler_params=_params(("arbitrary",)),
        name="combine",
    )(dest, x1, route, nfin, ys)


def _moe(x1, h2, route, wg, wu, wd, nfin, tril, td):
    n, d = x1.shape
    rank, cnt = _rank(route, tril)
    rows = MOE_ROWS if 2 * n >= N_EXPERTS * MOE_ROWS else MOE_ROWS_SMALL
    counts = cnt[0, :N_EXPERTS].astype(I32)
    pcounts = (counts + rows - 1) // rows * rows
    pends = jnp.cumsum(pcounts)
    pstarts = pends - pcounts
    nb = (2 * n) // rows + N_EXPERTS
    blk_e = jnp.minimum(
        jnp.sum(pends[None, :] <= (jnp.arange(nb, dtype=I32) * rows)[:, None], axis=1), N_EXPERTS - 1
    ).astype(I32)
    nused = (pends[-1:] // rows).astype(I32)
    pstart_row = jnp.pad(pstarts.astype(F32), (0, LANES - N_EXPERTS))[None, :]
    dest = _dest(route, rank, pstart_row, min(n, 1024))[:, 0:2].reshape(n // td, 1, 2 * td)
    xs = _dispatch(dest, h2, jnp.zeros((nb * rows, d), F32), td)
    ys = _ffn(blk_e, nused, xs, wg, wu, wd, rows)
    return _combine(dest, x1, route, nfin, ys, td)


def _layer(x, past_k, past_v, s0, conv_buf, wts, consts, *, prompt):
    (norm_mix, w_in_parts, bias, conv_w, alog, dtb, gnorm, wa, wb, wo, norm_ffn, wrh, wrl,
     wg, wu, wd, norm_final) = wts
    batch, seq, d = x.shape
    n = batch * seq
    left = LEFT_CHUNKS * CHUNK
    tm = 512 if n % 512 == 0 else n
    xf = x.reshape(n, d)
    keep = min(left, seq)
    assert (seq % tm == 0 and tm == keep) or (tm % seq == 0 and keep == seq)
    tiles_per_seq = max(seq // tm, 1)
    qkv, kv, xbc, gb, ab, gate = _inproj(xf, norm_mix, w_in_parts, tm, tiles_per_seq)

    if prompt:
        o_a = _attention(qkv, qkv, qkv, bias, batch=batch, seq=seq, cpb=LEFT_CHUNKS, prev_is_cache=False)
        rows = left
        s_init = jnp.zeros((batch, N_PAIRS, HEAD_DIM, LANES), F32)
        tail = jnp.zeros((batch, CONV_TAIL, 3 * WIDTH), F32)
    else:
        cache = jnp.concatenate(
            [past_k.reshape(batch * left, WIDTH), past_v.reshape(batch * left, WIDTH)], axis=1).astype(BF16)
        o_a = _attention(qkv, cache, qkv, bias, batch=batch, seq=seq, cpb=seq // CHUNK, prev_is_cache=True)
        rows = seq
        s_init = s0.reshape(batch, N_PAIRS, 2, HEAD_DIM, HEAD_DIM).transpose(0, 1, 3, 2, 4).reshape(
            batch, N_PAIRS, HEAD_DIM, LANES)
        tail = jnp.pad(conv_buf, ((0, 0), (CONV_TAIL - (CONV_W - 1), 0), (0, 0)))

    bd, eb, eg, lt, eye2, tril = consts
    o_b, s_new = _gdn(xbc, gb, ab, conv_w, alog, dtb, gnorm, s_init, tail,
                      (bd, eb, eg, lt, eye2), batch=batch, seq=seq, rows=rows)

    x1, h2, route = _post(o_a, o_b, gate, xf, wa, wb, wo, norm_ffn, wrh, wrl, tm)
    y = _moe(x1, h2, route, wg, wu, wd, norm_final, tril, td=min(512, n))

    kv4 = kv.reshape(batch, keep, 2, N_HEADS, HEAD_DIM)
    new_k = kv4[:, :, 0]
    new_v = kv4[:, :, 1]
    new_s = s_new.reshape(batch, N_PAIRS, HEAD_DIM, 2, HEAD_DIM).transpose(0, 1, 3, 2, 4).reshape(
        batch, N_HEADS, HEAD_DIM, HEAD_DIM)
    new_conv = xbc.reshape(batch, seq, 3 * WIDTH)[:, seq - (CONV_W - 1):]
    return y.reshape(batch, seq, d), new_k, new_v, new_s, new_conv


def _constants():
    lane = jnp.arange(WIDTH)
    src = jnp.arange(LANES)
    bd = (src[:, None] // HEAD_DIM == src[None, :] // HEAD_DIM).astype(BF16)
    eb = (src[:, None] == lane[None, :] // HEAD_DIM).astype(BF16)
    eg = (src[:, None] == N_HEADS + lane[None, :] // HEAD_DIM).astype(BF16)
    c = jnp.arange(CHUNK)
    lt = (c[:, None] >= c[None, :]).astype(BF16)
    eye2 = (c[:, None] == jnp.arange(LANES)[None, :] % HEAD_DIM).astype(F32)
    t = jnp.arange(512)
    tril = (t[:, None] > t[None, :]).astype(BF16)
    return bd, eb, eg, lt, eye2, tril


def kernel(x_prompt, x_sample, cache_attn_k, cache_attn_v, state_delta, state_conv, norm_mix, w_in, rel_bias,
           conv_w, a_log, dt_bias, gdn_norm, w_branch_a, w_branch_b, w_out, norm_ffn, w_route_group,
           w_route_expert, w_gate, w_up, w_down, norm_final):
    depth = w_in.shape[0]
    d = x_prompt.shape[-1]
    left = LEFT_CHUNKS * CHUNK
    consts = _constants()

    def bias_table(rb):
        nd = BAND + CHUNK - 1
        d_rev = (left + CHUNK - 1) - jnp.arange(nd)
        vr = rb[:, jnp.clip(d_rev, -REL_CLIP, REL_CLIP) + REL_CLIP]
        return jnp.stack([vr[:, CHUNK - 1 - i:CHUNK - 1 - i + BAND] for i in range(CHUNK)], axis=1)

    xp, xs = x_prompt, x_sample
    outs_p, outs_s = [], []
    for l in range(depth):
        w = w_in[l]
        o = 0
        parts = []
        for width in (3 * WIDTH, 3 * WIDTH, WIDTH, 2 * N_HEADS, 2 * d):
            parts.append(w[:, o:o + width])
            o += width
        wqkv, wxbc, wgb, wab, wgate = parts
        wqkv = jnp.concatenate([wqkv[:, :WIDTH] * (HEAD_DIM ** -0.5), wqkv[:, WIDTH:]], axis=1)
        wab = jnp.pad(wab, ((0, 0), (0, LANES - 2 * N_HEADS)))
        w_in_parts = tuple(m.astype(BF16) for m in (wqkv, wxbc, wgb, wab, wgate))
        lane_pad = lambda v: jnp.pad(v[None, :], ((0, 0), (N_HEADS, LANES - 2 * N_HEADS)))
        wr = jnp.pad(jnp.concatenate([w_route_group[l], w_route_expert[l]], axis=1),
                     ((0, 0), (0, LANES - N_GROUPS - N_EXPERTS)))
        wrh = wr.astype(BF16)
        wrl = (wr - wrh.astype(F32)).astype(BF16)
        wts = (
            norm_mix[l][None, :], w_in_parts, bias_table(rel_bias[l]), conv_w[l], lane_pad(a_log[l]),
            lane_pad(dt_bias[l]), jnp.tile(gdn_norm[l], N_HEADS)[None, :],
            w_branch_a[l].astype(BF16), w_branch_b[l].astype(BF16), w_out[l].astype(BF16),
            norm_ffn[l][None, :], wrh, wrl,
            w_gate[l], w_up[l], w_down[l], norm_final[None, :],
        )
        assert depth == 1
        xp, kp, vp, sp, cp = _layer(xp, None, None, None, None, wts, consts, prompt=True)
        xs, ks, vs, ss, cs = _layer(xs, cache_attn_k[l], cache_attn_v[l], state_delta[l], state_conv[l],
                                    wts, consts, prompt=False)
        outs_p.append((kp, vp, sp, cp))
        outs_s.append((ks, vs, ss, cs))
    stack = lambda items, idx: jnp.stack([it[idx] for it in items])
    return (xp, xs,
            stack(outs_p, 0), stack(outs_p, 1), stack(outs_p, 2), stack(outs_p, 3),
            stack(outs_s, 0), stack(outs_s, 1), stack(outs_s, 2), stack(outs_s, 3))
```

```python
import functools

import jax
import jax.numpy as jnp
from jax import lax
from jax.experimental import pallas as pl
from jax.experimental.pallas import tpu as pltpu

F32 = jnp.float32
BF16 = jnp.bfloat16
I32 = jnp.int32

RMS_EPS = 1e-6
CHUNK = 64
LEFT_CHUNKS = 8
BAND = (LEFT_CHUNKS + 1) * CHUNK
N_HEADS = 8
HEAD_DIM = 64
N_PAIRS = N_HEADS // 2
WIDTH = N_HEADS * HEAD_DIM
REL_CLIP = 128
CONV_W = 4
N_GROUPS = 4
EXPERTS_PER_GROUP = 8
N_EXPERTS = N_GROUPS * EXPERTS_PER_GROUP
LANES = 128
CONV_TAIL = 8
MOE_ROWS = 512
MOE_ROWS_SMALL = 128
ROW_UNROLL = 8
VMEM_LIMIT = 56 * 1024 * 1024


def _dot(a, b):
    return jnp.dot(a, b, preferred_element_type=F32)


def _dot_nt(a, b):
    return lax.dot_general(a, b, (((1,), (1,)), ((), ())), preferred_element_type=F32)


def _dot_tn(a, b):
    return lax.dot_general(a, b, (((0,), (0,)), ((), ())), preferred_element_type=F32)


def _sigmoid(x):
    return 1.0 / (1.0 + jnp.exp(-x))


def _split2(x):
    hi = x.astype(BF16)
    lo = (x - hi.astype(F32)).astype(BF16)
    return hi, lo


def _split3(x):
    hi = x.astype(BF16)
    r = x - hi.astype(F32)
    mid = r.astype(BF16)
    lo = (r - mid.astype(F32)).astype(BF16)
    return hi, mid, lo


def _dot3_rhs01(x, mat):
    hi, mid, lo = _split3(x)
    return (_dot(hi, mat) + _dot(mid, mat)) + _dot(lo, mat)


def _dot3_lhs01(mat, x):
    hi, mid, lo = _split3(x)
    return (_dot(mat, hi) + _dot(mat, mid)) + _dot(mat, lo)


def _group_sum(x2, bd):
    hi, lo = _split2(x2)
    outs = []
    for p in range(x2.shape[1] // LANES):
        cols = slice(p * LANES, (p + 1) * LANES)
        outs.append(_dot(hi[:, cols], bd) + _dot(lo[:, cols], bd))
    return jnp.concatenate(outs, axis=1)


def _params(sem):
    return pltpu.CompilerParams(dimension_semantics=sem, vmem_limit_bytes=VMEM_LIMIT)


def _conv_silu(x, w, tail_scr, xc_scr):
    rows = x.shape[0]
    xc_scr[0:CONV_TAIL, :] = tail_scr[...]
    xc_scr[CONV_TAIL:, :] = x
    y = xc_scr[CONV_TAIL - 3:CONV_TAIL - 3 + rows, :] * w[0:1]
    y = y + xc_scr[CONV_TAIL - 2:CONV_TAIL - 2 + rows, :] * w[1:2]
    y = y + xc_scr[CONV_TAIL - 1:CONV_TAIL - 1 + rows, :] * w[2:3]
    y = y + x * w[3:4]
    tail_scr[...] = x[rows - CONV_TAIL:rows]
    return y * _sigmoid(y)


def _inproj_kernel(x_ref, g_ref, wqkv_ref, wxbc_ref, wgb_ref, wab_ref, wgate_ref,
                   qkv_ref, kv_ref, xbc_ref, gb_ref, ab_ref, gate_ref, *, tiles_per_seq):
    x = x_ref[...]
    h = x * lax.rsqrt(jnp.mean(x * x, axis=-1, keepdims=True) + RMS_EPS) * g_ref[...]
    hb = h.astype(BF16)
    qkv_ref[...] = _dot(hb, wqkv_ref[...]).astype(BF16)

    @pl.when(pl.program_id(0) % tiles_per_seq == tiles_per_seq - 1)
    def _():
        kv_ref[...] = _dot(hb, wqkv_ref[:, WIDTH:])

    xbc_ref[...] = _dot(hb, wxbc_ref[...])
    gb_ref[...] = _dot(hb, wgb_ref[...])
    ab_ref[...] = _dot(hb, wab_ref[...])
    gate_ref[...] = _dot(hb, wgate_ref[...])


def _inproj(x, g, w, tm, tiles_per_seq):
    n, d = x.shape
    wqkv, wxbc, wgb, wab, wgate = w
    row = lambda i: (i, 0)
    tail = lambda i: (i // tiles_per_seq, 0)
    const = lambda i: (0, 0)
    widths = (3 * WIDTH, 2 * WIDTH, 3 * WIDTH, WIDTH, LANES, 2 * d)
    dtypes = (BF16, F32, F32, F32, F32, F32)
    nrows = (n, n // tiles_per_seq, n, n, n, n)
    maps = (row, tail, row, row, row, row)
    return pl.pallas_call(
        functools.partial(_inproj_kernel, tiles_per_seq=tiles_per_seq),
        grid=(n // tm,),
        in_specs=[pl.BlockSpec((tm, d), row), pl.BlockSpec((1, d), const)]
        + [pl.BlockSpec(m.shape, const) for m in w],
        out_specs=[pl.BlockSpec((tm, c), m) for c, m in zip(widths, maps)],
        out_shape=[jax.ShapeDtypeStruct((r, c), t) for r, c, t in zip(nrows, widths, dtypes)],
        compiler_params=_params(("arbitrary",)),
        name="inproj",
    )(x, g, wqkv, wxbc, wgb, wab, wgate)


def _attn_kernel(q_ref, kp_ref, kc_ref, vp_ref, vc_ref, bias_ref, o_ref, kw_ref, vw_ref, s_ref, p_ref,
                 *, cpb, mask_first):
    left = LEFT_CHUNKS * CHUNK
    kw_ref[0:left, :] = kp_ref[...]
    kw_ref[left:, :] = kc_ref[...]
    vw_ref[0:left, :] = vp_ref[...]
    vw_ref[left:, :] = vc_ref[...]
    lane = lax.broadcasted_iota(I32, (CHUNK, LANES), 1)
    m0 = lane < HEAD_DIM
    first_block = pl.program_id(1) == 0

    def chunk(i, carry):
        r0 = pl.multiple_of(i * CHUNK, CHUNK)
        if mask_first:
            col = lax.broadcasted_iota(I32, (CHUNK, BAND), 1) + r0
            valid = jnp.logical_or(jnp.logical_not(first_block), col >= left)
        for p in range(N_PAIRS):
            cols = slice(p * LANES, (p + 1) * LANES)
            q = q_ref[pl.ds(r0, CHUNK), cols]
            k = kw_ref[pl.ds(r0, BAND), cols]
            for hh in range(2):
                qm = jnp.where(m0 if hh == 0 else jnp.logical_not(m0), q, jnp.zeros_like(q))
                s = _dot_nt(qm, k) + bias_ref[2 * p + hh]
                if mask_first:
                    s = jnp.where(valid, s, -1e30)
                s_ref[2 * p + hh] = s
        for h in range(N_HEADS):
            s = s_ref[h]
            e = jnp.exp(s - jnp.max(s, axis=-1, keepdims=True))
            p_ref[h] = (e / jnp.sum(e, axis=-1, keepdims=True)).astype(BF16)
        pairs = []
        for p in range(N_PAIRS):
            v = vw_ref[pl.ds(r0, BAND), p * LANES:(p + 1) * LANES]
            pairs.append(jnp.where(m0, _dot(p_ref[2 * p], v), _dot(p_ref[2 * p + 1], v)))
        o_ref[pl.ds(r0, CHUNK), :] = jnp.concatenate(pairs, axis=1).astype(o_ref.dtype)
        return carry

    lax.fori_loop(0, cpb, chunk, 0)


def _attention(q_src, k_prev_src, kv_cur_src, bias, *, batch, seq, cpb, prev_is_cache):
    left = LEFT_CHUNKS * CHUNK
    rows = cpb * CHUNK
    nblk = seq // rows
    if prev_is_cache:
        prev_idx = lambda col: (lambda b, j: (b, col))
        kcol, vcol = 0, 1
    else:
        assert rows == left
        prev_idx = lambda col: (lambda b, j: (b * nblk + jnp.maximum(j - 1, 0), col))
        kcol, vcol = 1, 2
    cur = lambda col: (lambda b, j: (b * nblk + j, col))
    kern = functools.partial(_attn_kernel, cpb=cpb, mask_first=not prev_is_cache)
    return pl.pallas_call(
        kern,
        grid=(batch, nblk),
        in_specs=[
            pl.BlockSpec((rows, WIDTH), cur(0)),
            pl.BlockSpec((left, WIDTH), prev_idx(kcol)),
            pl.BlockSpec((rows, WIDTH), cur(1)),
            pl.BlockSpec((left, WIDTH), prev_idx(vcol)),
            pl.BlockSpec((rows, WIDTH), cur(2)),
            pl.BlockSpec(bias.shape, lambda b, j: (0, 0, 0)),
        ],
        out_specs=pl.BlockSpec((rows, WIDTH), lambda b, j: (b * nblk + j, 0)),
        out_shape=jax.ShapeDtypeStruct((batch * seq, WIDTH), BF16),
        scratch_shapes=[pltpu.VMEM((left + rows, WIDTH), BF16), pltpu.VMEM((left + rows, WIDTH), BF16),
                        pltpu.VMEM((N_HEADS, CHUNK, BAND), F32), pltpu.VMEM((N_HEADS, CHUNK, BAND), BF16)],
        compiler_params=_params(("parallel", "arbitrary")),
        name="attn",
    )(q_src, k_prev_src, kv_cur_src, k_prev_src, kv_cur_src, bias)


def _blockdiag(b, m0):
    z = jnp.zeros_like(b)
    return jnp.concatenate([jnp.where(m0, b, z), jnp.where(m0, z, b)], axis=0).astype(BF16)


def _pairdot(a, b, m0):
    return _dot(a.astype(BF16), _blockdiag(b, m0))


def _gdn_kernel(xbc_ref, gb_ref, ab_ref, convw_ref, alog_ref, dtb_ref, gnorm_ref, s0_ref, tail0_ref,
                bd_ref, eb_ref, eg_ref, lt_ref, eye2_ref,
                ob_ref, sout_ref,
                s_scr, tail_scr, xc_scr, kdt_scr, u_scr, q_scr, k_scr, v_scr, beta_scr, gc_scr, o_scr, pw_scr, t_scr, aqk_scr,
                *, rows):
    nchunks = rows // CHUNK

    ri = lax.broadcasted_iota(I32, (CHUNK, LANES), 0)
    ci = lax.broadcasted_iota(I32, (CHUNK, LANES), 1)
    m0 = ci < HEAD_DIM

    @pl.when(pl.program_id(1) == 0)
    def _():
        for p in range(N_PAIRS):
            s = s0_ref[0, p]
            z = jnp.zeros_like(s)
            s_scr[p] = jnp.concatenate([jnp.where(m0, s, z), jnp.where(m0, z, s)], axis=0)
        tail_scr[...] = tail0_ref[0]

    y = _conv_silu(xbc_ref[...], convw_ref[...], tail_scr, xc_scr)

    bd = bd_ref[...]
    q = y[:, 0:WIDTH]
    k = y[:, WIDTH:2 * WIDTH]
    q_scr[...] = q * lax.rsqrt(_group_sum(q * q, bd) + RMS_EPS) * (HEAD_DIM ** -0.5)
    k_scr[...] = k * lax.rsqrt(_group_sum(k * k, bd) + RMS_EPS)
    v_scr[...] = y[:, 2 * WIDTH:3 * WIDTH]

    ab = ab_ref[...]
    z = ab + dtb_ref[...]
    softplus = jnp.maximum(z, 0.0) + jnp.log1p(jnp.exp(-jnp.abs(z)))
    g = -jnp.exp(alog_ref[...]) * softplus
    beta_scr[...] = _dot3_rhs01(_sigmoid(ab), eb_ref[...])
    gexp = _dot3_rhs01(g, eg_ref[...])
    lt = lt_ref[...]
    for c in range(nchunks):
        gc_scr[c * CHUNK:(c + 1) * CHUNK, :] = _dot3_lhs01(lt, gexp[c * CHUNK:(c + 1) * CHUNK])

    cj = jnp.bitwise_and(ci, HEAD_DIM - 1)
    incl = ri >= cj
    strict = ri > cj
    eye2 = eye2_ref[...]
    ones = jnp.ones((CHUNK, CHUNK), BF16)
    r2 = lax.broadcasted_iota(I32, (LANES, LANES), 0)
    c2 = lax.broadcasted_iota(I32, (LANES, LANES), 1)
    eye128 = jnp.where(r2 == c2, 1.0, 0.0).astype(BF16)
    same_head = (r2 < HEAD_DIM) == (c2 < HEAD_DIM)

    def tiles(body, unroll):
        def step(c, carry):
            if isinstance(c, int):
                rs = slice(c * CHUNK, (c + 1) * CHUNK)
            else:
                rs = pl.ds(pl.multiple_of(c * CHUNK, CHUNK), CHUNK)
            for p in range(N_PAIRS):
                body(rs, slice(p * LANES, (p + 1) * LANES), c * N_PAIRS + p, p)
            return carry
        if nchunks <= unroll:
            for c in range(nchunks):
                step(c, 0)
        else:
            lax.fori_loop(0, nchunks, step, 0, unroll=unroll)

    def setup(rs, cols, tile, p):
        qp = q_scr[rs, cols]
        kp = k_scr[rs, cols]
        beta = beta_scr[rs, cols]
        gc = gc_scr[rs, cols]
        gl = gc[CHUNK - 1:CHUNK, :]
        eg = jnp.exp(gc)
        kb = kp * beta
        bdk = _blockdiag(kp, m0)
        kk = _dot_nt(kb.astype(BF16), bdk)
        qk = _dot_nt(qp.astype(BF16), bdk)
        gcol = _dot3_lhs01(ones, gc * eye2)
        decay = jnp.where(incl, jnp.exp(jnp.where(incl, gc - gcol, 0.0)), 0.0)
        nmat = -(kk * jnp.where(strict, decay, 0.0))
        pw_scr[rs, cols] = nmat
        t_scr[rs, cols] = eye2 + nmat
        aqk_scr[rs, cols] = qk * decay
        q_scr[rs, cols] = qp * eg
        kd = (kp * jnp.exp(gl - gc)).astype(BF16)
        kdt_scr[tile] = _dot_nt(eye128, kd).astype(BF16)
        v_scr[rs, cols] = v_scr[rs, cols] * beta
        beta_scr[rs, cols] = kb * eg

    tiles(setup, 8)

    def double(rs, cols, tile, p):
        pw = pw_scr[rs, cols]
        pw = _pairdot(pw, pw, m0)
        pw_scr[rs, cols] = pw
        t = t_scr[rs, cols]
        t_scr[rs, cols] = t + _pairdot(t, pw, m0)

    for _ in range(5):
        tiles(double, 8)

    def solve(rs, cols, tile, p):
        t = t_scr[rs, cols]
        v_scr[rs, cols] = _pairdot(t, v_scr[rs, cols], m0)
        beta_scr[rs, cols] = _pairdot(t, beta_scr[rs, cols], m0)

    tiles(solve, 8)

    def scan_u(rs, cols, tile, p):
        u_scr[p] = v_scr[rs, cols] - _dot(beta_scr[rs, cols].astype(BF16), s_scr[p].astype(BF16))

    def scan_s(rs, cols, tile, p):
        s = s_scr[p]
        u = u_scr[p]
        o_scr[rs, cols] = (_dot(q_scr[rs, cols].astype(BF16), s.astype(BF16))
                           + _pairdot(aqk_scr[rs, cols], u, m0))
        f = _dot(kdt_scr[tile], u.astype(BF16))
        gt = jnp.exp(gc_scr[rs, cols][CHUNK - 1:CHUNK, :])
        s_scr[p] = s * gt + jnp.where(same_head, f, 0.0)

    def scan(c, carry):
        rs = pl.ds(pl.multiple_of(c * CHUNK, CHUNK), CHUNK)
        for body in (scan_u, scan_s):
            for p in range(N_PAIRS):
                body(rs, slice(p * LANES, (p + 1) * LANES), c * N_PAIRS + p, p)
        return carry

    lax.fori_loop(0, nchunks, scan, 0)

    o = o_scr[...]
    on = o * lax.rsqrt(_group_sum(o * o, bd) * (1.0 / HEAD_DIM) + RMS_EPS) * gnorm_ref[...]
    gate = gb_ref[...]
    ob_ref[...] = (on * (gate * _sigmoid(gate))).astype(ob_ref.dtype)
    for p in range(N_PAIRS):
        s = s_scr[p]
        sout_ref[0, p] = jnp.where(m0, s[0:HEAD_DIM], s[HEAD_DIM:2 * HEAD_DIM])


def _gdn(xbc, gb, ab, convw, alog, dtb, gnorm, s0, tail0, consts, *, batch, seq, rows):
    nblk = seq // rows
    bd, eb, eg, lt, eye2 = consts
    blk = lambda b, j: (b * nblk + j, 0)
    c2 = lambda b, j: (0, 0)
    per_b3 = lambda b, j: (b, 0, 0)
    per_b4 = lambda b, j: (b, 0, 0, 0)
    kern = functools.partial(_gdn_kernel, rows=rows)
    wide = pltpu.VMEM((rows, WIDTH), F32)
    return pl.pallas_call(
        kern,
        grid=(batch, nblk),
        in_specs=[
            pl.BlockSpec((rows, 3 * WIDTH), blk),
            pl.BlockSpec((rows, WIDTH), blk),
            pl.BlockSpec((rows, LANES), blk),
            pl.BlockSpec(convw.shape, c2),
            pl.BlockSpec(alog.shape, c2),
            pl.BlockSpec(dtb.shape, c2),
            pl.BlockSpec(gnorm.shape, c2),
            pl.BlockSpec((1, N_PAIRS, HEAD_DIM, LANES), per_b4),
            pl.BlockSpec((1, CONV_TAIL, 3 * WIDTH), per_b3),
            pl.BlockSpec(bd.shape, c2),
            pl.BlockSpec(eb.shape, c2),
            pl.BlockSpec(eg.shape, c2),
            pl.BlockSpec(lt.shape, c2),
            pl.BlockSpec(eye2.shape, c2),
        ],
        out_specs=[
            pl.BlockSpec((rows, WIDTH), blk),
            pl.BlockSpec((1, N_PAIRS, HEAD_DIM, LANES), per_b4),
        ],
        out_shape=[
            jax.ShapeDtypeStruct((batch * seq, WIDTH), BF16),
            jax.ShapeDtypeStruct((batch, N_PAIRS, HEAD_DIM, LANES), F32),
        ],
        scratch_shapes=[
            pltpu.VMEM((N_PAIRS, LANES, LANES), F32),
            pltpu.VMEM((CONV_TAIL, 3 * WIDTH), F32),
            pltpu.VMEM((CONV_TAIL + rows, 3 * WIDTH), F32),
            pltpu.VMEM((rows // CHUNK * N_PAIRS, LANES, HEAD_DIM), BF16),
            pltpu.VMEM((N_PAIRS, CHUNK, LANES), F32),
            wide, wide, wide, wide, wide, wide, wide, wide, wide,
        ],
        compiler_params=_params(("parallel", "arbitrary")),
        name="gdn",
    )(xbc, gb, ab, convw, alog, dtb, gnorm, s0, tail0, bd, eb, eg, lt, eye2)


def _post_kernel(oa_ref, ob_ref, gate_ref, x_ref, wa_ref, wb_ref, wo_ref, nf_ref, wrh_ref, wrl_ref,
                 x1_ref, h2_ref, route_ref):
    d = x_ref.shape[1]
    a = _dot(oa_ref[...], wa_ref[...])
    b = _dot(ob_ref[...], wb_ref[...])
    merged = _sigmoid(gate_ref[:, 0:d]) * a + _sigmoid(gate_ref[:, d:2 * d]) * b
    x1 = x_ref[...] + _dot(merged.astype(BF16), wo_ref[...])
    x1_ref[...] = x1
    h2 = x1 * lax.rsqrt(jnp.mean(x1 * x1, axis=-1, keepdims=True) + RMS_EPS) * nf_ref[...]
    h2_ref[...] = h2

    hi, lo = _split2(h2)
    wrh = wrh_ref[...]
    logits = _dot(hi, wrh) + (_dot(lo, wrh) + _dot(hi, wrl_ref[...]))

    lane = lax.broadcasted_iota(I32, logits.shape, 1).astype(F32)
    big = float(LANES)
    ninf = -jnp.inf
    lg = jnp.where(lane < N_GROUPS, logits, ninf)
    mg = jnp.max(lg, axis=-1, keepdims=True)
    grp = jnp.min(jnp.where(lg == mg, lane, big), axis=-1, keepdims=True)
    p_grp = 1.0 / jnp.sum(jnp.exp(lg - mg), axis=-1, keepdims=True)
    lo_lane = N_GROUPS + grp * EXPERTS_PER_GROUP
    le = jnp.where(jnp.logical_and(lane >= lo_lane, lane < lo_lane + EXPERTS_PER_GROUP), logits, ninf)
    m1 = jnp.max(le, axis=-1, keepdims=True)
    i1 = jnp.min(jnp.where(le == m1, lane, big), axis=-1, keepdims=True)
    le2 = jnp.where(lane == i1, ninf, le)
    m2 = jnp.max(le2, axis=-1, keepdims=True)
    i2 = jnp.min(jnp.where(le2 == m2, lane, big), axis=-1, keepdims=True)
    e2 = jnp.exp(m2 - m1)
    w1 = 1.0 / (1.0 + e2) * p_grp
    w2 = e2 / (1.0 + e2) * p_grp
    out = jnp.where(lane == 0, i1 - N_GROUPS, 0.0)
    out = jnp.where(lane == 1, i2 - N_GROUPS, out)
    out = jnp.where(lane == 2, w1, out)
    out = jnp.where(lane == 3, w2, out)
    route_ref[...] = out


def _post(oa, ob, gate, x, wa, wb, wo, nf, wrh, wrl, tm):
    n, d = x.shape
    row = lambda i: (i, 0)
    const = lambda i: (0, 0)
    return pl.pallas_call(
        _post_kernel,
        grid=(n // tm,),
        in_specs=[
            pl.BlockSpec((tm, WIDTH), row), pl.BlockSpec((tm, WIDTH), row),
            pl.BlockSpec((tm, 2 * d), row), pl.BlockSpec((tm, d), row),
            pl.BlockSpec(wa.shape, const), pl.BlockSpec(wb.shape, const), pl.BlockSpec(wo.shape, const),
            pl.BlockSpec(nf.shape, const), pl.BlockSpec(wrh.shape, const), pl.BlockSpec(wrl.shape, const),
        ],
        out_specs=[pl.BlockSpec((tm, d), row), pl.BlockSpec((tm, d), row), pl.BlockSpec((tm, LANES), row)],
        out_shape=[
            jax.ShapeDtypeStruct((n, d), F32),
            jax.ShapeDtypeStruct((n, d), F32),
            jax.ShapeDtypeStruct((n, LANES), F32),
        ],
        compiler_params=_params(("parallel",)),
        name="post",
    )(oa, ob, gate, x, wa, wb, wo, nf, wrh, wrl)


def _rank_kernel(route_ref, tril_ref, rank_ref, cnt_ref, carry_scr):
    @pl.when(pl.program_id(0) == 0)
    def _():
        carry_scr[...] = jnp.zeros_like(carry_scr)

    r = route_ref[...]
    lane = lax.broadcasted_iota(I32, r.shape, 1)
    lanef = lane.astype(F32)
    oh1 = lanef == r[:, 0:1]
    oh2 = lanef == r[:, 1:2]
    oh = jnp.where(jnp.logical_or(oh1, oh2), 1.0, 0.0)
    before = _dot(tril_ref[...], oh.astype(BF16)) + carry_scr[...]
    rank1 = jnp.sum(jnp.where(oh1, before, 0.0), axis=-1, keepdims=True)
    rank2 = jnp.sum(jnp.where(oh2, before, 0.0), axis=-1, keepdims=True)
    rank_ref[...] = jnp.where(lane == 0, rank1, jnp.where(lane == 1, rank2, 0.0))
    carry_scr[...] = carry_scr[...] + jnp.sum(oh, axis=0, keepdims=True)
    cnt_ref[...] = carry_scr[...]


def _rank(route, tril):
    n = route.shape[0]
    tr = min(tril.shape[0], n)
    tril = tril[:tr, :tr]
    return pl.pallas_call(
        _rank_kernel,
        grid=(n // tr,),
        in_specs=[pl.BlockSpec((tr, LANES), lambda i: (i, 0)), pl.BlockSpec(tril.shape, lambda i: (0, 0))],
        out_specs=[pl.BlockSpec((tr, LANES), lambda i: (i, 0)), pl.BlockSpec((1, LANES), lambda i: (0, 0))],
        out_shape=[jax.ShapeDtypeStruct((n, LANES), F32), jax.ShapeDtypeStruct((1, LANES), F32)],
        scratch_shapes=[pltpu.VMEM((1, LANES), F32)],
        compiler_params=_params(("arbitrary",)),
        name="rank",
    )(route, tril)


def _dest_kernel(route_ref, rank_ref, pstart_ref, dest_ref):
    r = route_ref[...]
    rk = rank_ref[...]
    ps = pstart_ref[...]
    lane = lax.broadcasted_iota(I32, r.shape, 1)
    lanef = lane.astype(F32)
    d1 = jnp.sum(jnp.where(lanef == r[:, 0:1], ps, 0.0), axis=-1, keepdims=True) + rk[:, 0:1]
    d2 = jnp.sum(jnp.where(lanef == r[:, 1:2], ps, 0.0), axis=-1, keepdims=True) + rk[:, 1:2]
    dest_ref[...] = jnp.where(lane == 0, d1, jnp.where(lane == 1, d2, 0.0)).astype(I32)


def _dest(route, rank, pstart, tr):
    n = route.shape[0]
    row = lambda i: (i, 0)
    return pl.pallas_call(
        _dest_kernel,
        grid=(n // tr,),
        in_specs=[pl.BlockSpec((tr, LANES), row), pl.BlockSpec((tr, LANES), row),
                  pl.BlockSpec((1, LANES), lambda i: (0, 0))],
        out_specs=pl.BlockSpec((tr, LANES), row),
        out_shape=jax.ShapeDtypeStruct((n, LANES), I32),
        compiler_params=_params(("parallel",)),
        name="dest",
    )(route, rank, pstart)


def _row_copy(src_ref, src_row, dst_ref, dst_row, sem):
    return pltpu.make_async_copy(src_ref.at[pl.ds(src_row, 1)], dst_ref.at[pl.ds(dst_row, 1)], sem)


def _for_rows(td, per_row):
    def group(i, carry):
        for j in range(ROW_UNROLL):
            per_row(i * ROW_UNROLL + j, j)
        return carry
    lax.fori_loop(0, td // ROW_UNROLL, group, 0)


def _dispatch_kernel(zblk_ref, nused_ref, dest_ref, h_ref, xs_ref, zero_ref, sem, zsem, *, nb):
    td = h_ref.shape[0]
    rows = zero_ref.shape[0]

    @pl.when(pl.program_id(0) == 0)
    def _():
        zero_ref[...] = jnp.zeros_like(zero_ref)

        def zero_copy(blk):
            return pltpu.make_async_copy(zero_ref, xs_ref.at[pl.ds(pl.multiple_of(blk * rows, rows), rows)], zsem)

        def for_zero_blocks(act):
            for e in range(N_EXPERTS):
                @pl.when(zblk_ref[e] >= 0)
                def _():
                    act(zero_copy(zblk_ref[e]))

            def tail(blk, carry):
                act(zero_copy(blk))
                return carry
            lax.fori_loop(nused_ref[0], nb, tail, 0)

        for_zero_blocks(lambda cp: cp.start())
        for_zero_blocks(lambda cp: cp.wait())

    def issue(t, j):
        _row_copy(h_ref, t, xs_ref, dest_ref[0, 0, 2 * t], sem).start(priority=0)
        _row_copy(h_ref, t, xs_ref, dest_ref[0, 0, 2 * t + 1], sem).start(priority=1)

    _for_rows(td, issue)
    whole = pltpu.make_async_copy(h_ref, xs_ref.at[pl.ds(0, td)], sem)
    whole.wait()
    whole.wait()


def _dispatch(zblk, nused, dest, h2, td, rows, nb):
    n, d = h2.shape
    grid_spec = pltpu.PrefetchScalarGridSpec(
        num_scalar_prefetch=2,
        grid=(n // td,),
        in_specs=[
            pl.BlockSpec((1, 1, 2 * td), lambda i, zb, nu: (i, 0, 0), memory_space=pltpu.SMEM),
            pl.BlockSpec((td, d), lambda i, zb, nu: (i, 0)),
        ],
        out_specs=pl.BlockSpec(memory_space=pl.ANY),
        scratch_shapes=[pltpu.VMEM((rows, d), F32), pltpu.SemaphoreType.DMA(()), pltpu.SemaphoreType.DMA(())],
    )
    return pl.pallas_call(
        functools.partial(_dispatch_kernel, nb=nb),
        grid_spec=grid_spec,
        out_shape=jax.ShapeDtypeStruct((nb * rows, d), F32),
        compiler_params=_params(("arbitrary",)),
        name="dispatch",
    )(zblk, nused, dest, h2)


def _ffn_kernel(blk_e_ref, nused_ref, xs_ref, wg_ref, wu_ref, wd_ref, ys_ref):
    del blk_e_ref
    i = pl.program_id(0)

    @pl.when(i < nused_ref[0])
    def _():
        xb = xs_ref[...].astype(BF16)
        g = _dot(xb, wg_ref[0].astype(BF16))
        u = _dot(xb, wu_ref[0].astype(BF16))
        hb = (g * _sigmoid(g)) * u
        ys_ref[...] = _dot(hb.astype(BF16), wd_ref[0].astype(BF16))

    @pl.when(i >= nused_ref[0])
    def _():
        ys_ref[...] = jnp.zeros_like(ys_ref)


def _ffn(blk_e, nused, xs, wg, wu, wd, rows):
    npad, d = xs.shape
    f = wg.shape[2]
    nb = npad // rows
    grid_spec = pltpu.PrefetchScalarGridSpec(
        num_scalar_prefetch=2,
        grid=(nb,),
        in_specs=[
            pl.BlockSpec((rows, d), lambda i, be, nu: (i, 0)),
            pl.BlockSpec((1, d, f), lambda i, be, nu: (be[i], 0, 0)),
            pl.BlockSpec((1, d, f), lambda i, be, nu: (be[i], 0, 0)),
            pl.BlockSpec((1, f, d), lambda i, be, nu: (be[i], 0, 0)),
        ],
        out_specs=pl.BlockSpec((rows, d), lambda i, be, nu: (i, 0)),
    )
    return pl.pallas_call(
        _ffn_kernel,
        grid_spec=grid_spec,
        out_shape=jax.ShapeDtypeStruct((npad, d), F32),
        compiler_params=_params(("arbitrary",)),
        name="ffn",
    )(blk_e, nused, xs, wg, wu, wd)


def _combine_kernel(dest_ref, x1_ref, route_ref, nfin_ref, ys_ref, out_ref, g1_ref, g2_ref, sem):
    td = x1_ref.shape[0]

    def issue(t, j):
        _row_copy(ys_ref, dest_ref[0, 0, 2 * t], g1_ref, t, sem).start(priority=0)
        _row_copy(ys_ref, dest_ref[0, 0, 2 * t + 1], g2_ref, t, sem).start(priority=1)

    _for_rows(td, issue)
    pltpu.make_async_copy(ys_ref.at[pl.ds(0, td)], g1_ref, sem).wait()
    pltpu.make_async_copy(ys_ref.at[pl.ds(0, td)], g2_ref, sem).wait()

    r = route_ref[...]
    y = x1_ref[...] + (r[:, 2:3] * g1_ref[...] + r[:, 3:4] * g2_ref[...])
    out_ref[...] = y * lax.rsqrt(jnp.mean(y * y, axis=-1, keepdims=True) + RMS_EPS) * nfin_ref[...]


def _combine(dest, x1, route, nfin, ys, td):
    n, d = x1.shape
    return pl.pallas_call(
        _combine_kernel,
        grid=(n // td,),
        in_specs=[
            pl.BlockSpec((1, 1, 2 * td), lambda i: (i, 0, 0), memory_space=pltpu.SMEM),
            pl.BlockSpec((td, d), lambda i: (i, 0)),
            pl.BlockSpec((td, LANES), lambda i: (i, 0)),
            pl.BlockSpec((1, d), lambda i: (0, 0)),
            pl.BlockSpec(memory_space=pl.ANY),
        ],
        out_specs=pl.BlockSpec((td, d), lambda i: (i, 0)),
        out_shape=jax.ShapeDtypeStruct((n, d), F32),
        scratch_shapes=[pltpu.VMEM((td, d), F32), pltpu.VMEM((td, d), F32), pltpu.SemaphoreType.DMA(())],
        compiler_params=_params(("arbitrary",)),
        name="combine",
    )(dest, x1, route, nfin, ys)


def _moe(x1, h2, route, wg, wu, wd, nfin, tril, td):
    n, d = x1.shape
    rank, cnt = _rank(route, tril)
    rows = MOE_ROWS if 2 * n >= N_EXPERTS * MOE_ROWS else MOE_ROWS_SMALL
    counts = cnt[0, :N_EXPERTS].astype(I32)
    pcounts = (counts + rows - 1) // rows * rows
    pends = jnp.cumsum(pcounts)
    pstarts = pends - pcounts
    nb = (2 * n) // rows + N_EXPERTS
    blk_e = jnp.minimum(
        jnp.sum(pends[None, :] <= (jnp.arange(nb, dtype=I32) * rows)[:, None], axis=1), N_EXPERTS - 1
    ).astype(I32)
    nused = (pends[-1:] // rows).astype(I32)
    pstart_row = jnp.pad(pstarts.astype(F32), (0, LANES - N_EXPERTS))[None, :]
    dest = _dest(route, rank, pstart_row, min(n, 1024))[:, 0:2].reshape(n // td, 1, 2 * td)
    zblk = jnp.where(pcounts > 0, pends // rows - 1, -1).astype(I32)
    xs = _dispatch(zblk, nused, dest, h2, td, rows, nb)
    ys = _ffn(blk_e, nused, xs, wg, wu, wd, rows)
    return _combine(dest, x1, route, nfin, ys, td)


def _layer(x, past_k, past_v, s0, conv_buf, wts, consts, *, prompt):
    (norm_mix, w_in_parts, bias, conv_w, alog, dtb, gnorm, wa, wb, wo, norm_ffn, wrh, wrl,
     wg, wu, wd, norm_final) = wts
    batch, seq, d = x.shape
    n = batch * seq
    left = LEFT_CHUNKS * CHUNK
    tm = 512 if n % 512 == 0 else n
    xf = x.reshape(n, d)
    keep = min(left, seq)
    assert (seq % tm == 0 and tm == keep) or (tm % seq == 0 and keep == seq)
    tiles_per_seq = max(seq // tm, 1)
    qkv, kv, xbc, gb, ab, gate = _inproj(xf, norm_mix, w_in_parts, tm, tiles_per_seq)

    if prompt:
        o_a = _attention(qkv, qkv, qkv, bias, batch=batch, seq=seq, cpb=LEFT_CHUNKS, prev_is_cache=False)
        rows = left
        s_init = jnp.zeros((batch, N_PAIRS, HEAD_DIM, LANES), F32)
        tail = jnp.zeros((batch, CONV_TAIL, 3 * WIDTH), F32)
    else:
        cache = jnp.concatenate(
            [past_k.reshape(batch * left, WIDTH), past_v.reshape(batch * left, WIDTH)], axis=1).astype(BF16)
        o_a = _attention(qkv, cache, qkv, bias, batch=batch, seq=seq, cpb=seq // CHUNK, prev_is_cache=True)
        rows = seq
        s_init = s0.reshape(batch, N_PAIRS, 2, HEAD_DIM, HEAD_DIM).transpose(0, 1, 3, 2, 4).reshape(
            batch, N_PAIRS, HEAD_DIM, LANES)
        tail = jnp.pad(conv_buf, ((0, 0), (CONV_TAIL - (CONV_W - 1), 0), (0, 0)))

    bd, eb, eg, lt, eye2, tril = consts
    o_b, s_new = _gdn(xbc, gb, ab, conv_w, alog, dtb, gnorm, s_init, tail,
                      (bd, eb, eg, lt, eye2), batch=batch, seq=seq, rows=rows)

    x1, h2, route = _post(o_a, o_b, gate, xf, wa, wb, wo, norm_ffn, wrh, wrl, tm)
    y = _moe(x1, h2, route, wg, wu, wd, norm_final, tril, td=min(1024, n))

    kv4 = kv.reshape(batch, keep, 2, N_HEADS, HEAD_DIM)
    new_k = kv4[:, :, 0]
    new_v = kv4[:, :, 1]
    new_s = s_new.reshape(batch, N_PAIRS, HEAD_DIM, 2, HEAD_DIM).transpose(0, 1, 3, 2, 4).reshape(
        batch, N_HEADS, HEAD_DIM, HEAD_DIM)
    new_conv = xbc.reshape(batch, seq, 3 * WIDTH)[:, seq - (CONV_W - 1):]
    return y.reshape(batch, seq, d), new_k, new_v, new_s, new_conv


def _constants():
    lane = jnp.arange(WIDTH)
    src = jnp.arange(LANES)
    bd = (src[:, None] // HEAD_DIM == src[None, :] // HEAD_DIM).astype(BF16)
    eb = (src[:, None] == lane[None, :] // HEAD_DIM).astype(BF16)
    eg = (src[:, None] == N_HEADS + lane[None, :] // HEAD_DIM).astype(BF16)
    c = jnp.arange(CHUNK)
    lt = (c[:, None] >= c[None, :]).astype(BF16)
    eye2 = (c[:, None] == jnp.arange(LANES)[None, :] % HEAD_DIM).astype(F32)
    t = jnp.arange(512)
    tril = (t[:, None] > t[None, :]).astype(BF16)
    return bd, eb, eg, lt, eye2, tril


def kernel(x_prompt, x_sample, cache_attn_k, cache_attn_v, state_delta, state_conv, norm_mix, w_in, rel_bias,
           conv_w, a_log, dt_bias, gdn_norm, w_branch_a, w_branch_b, w_out, norm_ffn, w_route_group,
           w_route_expert, w_gate, w_up, w_down, norm_final):
    depth = w_in.shape[0]
    d = x_prompt.shape[-1]
    left = LEFT_CHUNKS * CHUNK
    consts = _constants()

    def bias_table(rb):
        nd = BAND + CHUNK - 1
        d_rev = (left + CHUNK - 1) - jnp.arange(nd)
        vr = rb[:, jnp.clip(d_rev, -REL_CLIP, REL_CLIP) + REL_CLIP]
        return jnp.stack([vr[:, CHUNK - 1 - i:CHUNK - 1 - i + BAND] for i in range(CHUNK)], axis=1)

    xp, xs = x_prompt, x_sample
    outs_p, outs_s = [], []
    for l in range(depth):
        w = w_in[l]
        o = 0
        parts = []
        for width in (3 * WIDTH, 3 * WIDTH, WIDTH, 2 * N_HEADS, 2 * d):
            parts.append(w[:, o:o + width])
            o += width
        wqkv, wxbc, wgb, wab, wgate = parts
        wqkv = jnp.concatenate([wqkv[:, :WIDTH] * (HEAD_DIM ** -0.5), wqkv[:, WIDTH:]], axis=1)
        wab = jnp.pad(wab, ((0, 0), (0, LANES - 2 * N_HEADS)))
        w_in_parts = tuple(m.astype(BF16) for m in (wqkv, wxbc, wgb, wab, wgate))
        lane_pad = lambda v: jnp.pad(v[None, :], ((0, 0), (N_HEADS, LANES - 2 * N_HEADS)))
        wr = jnp.pad(jnp.concatenate([w_route_group[l], w_route_expert[l]], axis=1),
                     ((0, 0), (0, LANES - N_GROUPS - N_EXPERTS)))
        wrh = wr.astype(BF16)
        wrl = (wr - wrh.astype(F32)).astype(BF16)
        wts = (
            norm_mix[l][None, :], w_in_parts, bias_table(rel_bias[l]), conv_w[l], lane_pad(a_log[l]),
            lane_pad(dt_bias[l]), jnp.tile(gdn_norm[l], N_HEADS)[None, :],
            w_branch_a[l].astype(BF16), w_branch_b[l].astype(BF16), w_out[l].astype(BF16),
            norm_ffn[l][None, :], wrh, wrl,
            w_gate[l], w_up[l], w_down[l], norm_final[None, :],
        )
        assert depth == 1
        xp, kp, vp, sp, cp = _layer(xp, None, None, None, None, wts, consts, prompt=True)
        xs, ks, vs, ss, cs = _layer(xs, cache_attn_k[l], cache_attn_v[l], state_delta[l], state_conv[l],
                                    wts, consts, prompt=False)
        outs_p.append((kp, vp, sp, cp))
        outs_s.append((ks, vs, ss, cs))
    stack = lambda items, idx: jnp.stack([it[idx] for it in items])
    return (xp, xs,
            stack(outs_p, 0), stack(outs_p, 1), stack(outs_p, 2), stack(outs_p, 3),
            stack(outs_s, 0), stack(outs_s, 1), stack(outs_s, 2), stack(outs_s, 3))
```

```python
import functools

import jax
import jax.numpy as jnp
from jax import lax
from jax.experimental import pallas as pl
from jax.experimental.pallas import tpu as pltpu

F32 = jnp.float32
BF16 = jnp.bfloat16
I32 = jnp.int32

RMS_EPS = 1e-6
CHUNK = 64
LEFT_CHUNKS = 8
BAND = (LEFT_CHUNKS + 1) * CHUNK
N_HEADS = 8
HEAD_DIM = 64
N_PAIRS = N_HEADS // 2
WIDTH = N_HEADS * HEAD_DIM
REL_CLIP = 128
CONV_W = 4
N_GROUPS = 4
EXPERTS_PER_GROUP = 8
N_EXPERTS = N_GROUPS * EXPERTS_PER_GROUP
LANES = 128
CONV_TAIL = 8
MOE_ROWS = 512
MOE_ROWS_SMALL = 128
ROW_UNROLL = 8
VMEM_LIMIT = 56 * 1024 * 1024


def _dot(a, b):
    return jnp.dot(a, b, preferred_element_type=F32)


def _dot_nt(a, b):
    return lax.dot_general(a, b, (((1,), (1,)), ((), ())), preferred_element_type=F32)


def _dot_tn(a, b):
    return lax.dot_general(a, b, (((0,), (0,)), ((), ())), preferred_element_type=F32)


def _sigmoid(x):
    return 1.0 / (1.0 + jnp.exp(-x))


def _split2(x):
    hi = x.astype(BF16)
    lo = (x - hi.astype(F32)).astype(BF16)
    return hi, lo


def _split3(x):
    hi = x.astype(BF16)
    r = x - hi.astype(F32)
    mid = r.astype(BF16)
    lo = (r - mid.astype(F32)).astype(BF16)
    return hi, mid, lo


def _dot3_rhs01(x, mat):
    hi, mid, lo = _split3(x)
    return (_dot(hi, mat) + _dot(mid, mat)) + _dot(lo, mat)


def _dot3_lhs01(mat, x):
    hi, mid, lo = _split3(x)
    return (_dot(mat, hi) + _dot(mat, mid)) + _dot(mat, lo)


def _group_sum(x2, bd):
    hi, lo = _split2(x2)
    outs = []
    for p in range(x2.shape[1] // LANES):
        cols = slice(p * LANES, (p + 1) * LANES)
        outs.append(_dot(hi[:, cols], bd) + _dot(lo[:, cols], bd))
    return jnp.concatenate(outs, axis=1)


def _params(sem):
    return pltpu.CompilerParams(dimension_semantics=sem, vmem_limit_bytes=VMEM_LIMIT)


def _conv_silu(x, w, tail_scr, xc_scr):
    rows = x.shape[0]
    xc_scr[0:CONV_TAIL, :] = tail_scr[...]
    xc_scr[CONV_TAIL:, :] = x
    y = xc_scr[CONV_TAIL - 3:CONV_TAIL - 3 + rows, :] * w[0:1]
    y = y + xc_scr[CONV_TAIL - 2:CONV_TAIL - 2 + rows, :] * w[1:2]
    y = y + xc_scr[CONV_TAIL - 1:CONV_TAIL - 1 + rows, :] * w[2:3]
    y = y + x * w[3:4]
    tail_scr[...] = x[rows - CONV_TAIL:rows]
    return y * _sigmoid(y)


def _inproj_kernel(x_ref, g_ref, wqkv_ref, wxbc_ref, wgb_ref, wab_ref, wgate_ref,
                   qkv_ref, kv_ref, xbc_ref, gb_ref, ab_ref, gate_ref, *, tiles_per_seq):
    x = x_ref[...]
    h = x * lax.rsqrt(jnp.mean(x * x, axis=-1, keepdims=True) + RMS_EPS) * g_ref[...]
    hb = h.astype(BF16)
    qkv_ref[...] = _dot(hb, wqkv_ref[...]).astype(BF16)

    @pl.when(pl.program_id(0) % tiles_per_seq == tiles_per_seq - 1)
    def _():
        kv_ref[...] = _dot(hb, wqkv_ref[:, WIDTH:])

    xbc_ref[...] = _dot(hb, wxbc_ref[...])
    gb_ref[...] = _dot(hb, wgb_ref[...])
    ab_ref[...] = _dot(hb, wab_ref[...])
    gate_ref[...] = _dot(hb, wgate_ref[...])


def _inproj(x, g, w, tm, tiles_per_seq):
    n, d = x.shape
    wqkv, wxbc, wgb, wab, wgate = w
    row = lambda i: (i, 0)
    tail = lambda i: (i // tiles_per_seq, 0)
    const = lambda i: (0, 0)
    widths = (3 * WIDTH, 2 * WIDTH, 3 * WIDTH, WIDTH, LANES, 2 * d)
    dtypes = (BF16, F32, F32, F32, F32, F32)
    nrows = (n, n // tiles_per_seq, n, n, n, n)
    maps = (row, tail, row, row, row, row)
    return pl.pallas_call(
        functools.partial(_inproj_kernel, tiles_per_seq=tiles_per_seq),
        grid=(n // tm,),
        in_specs=[pl.BlockSpec((tm, d), row), pl.BlockSpec((1, d), const)]
        + [pl.BlockSpec(m.shape, const) for m in w],
        out_specs=[pl.BlockSpec((tm, c), m) for c, m in zip(widths, maps)],
        out_shape=[jax.ShapeDtypeStruct((r, c), t) for r, c, t in zip(nrows, widths, dtypes)],
        compiler_params=_params(("arbitrary",)),
        name="inproj",
    )(x, g, wqkv, wxbc, wgb, wab, wgate)


def _attn_kernel(q_ref, kp_ref, kc_ref, vp_ref, vc_ref, bias_ref, o_ref, kw_ref, vw_ref, s_ref, p_ref,
                 *, cpb, mask_first):
    left = LEFT_CHUNKS * CHUNK
    kw_ref[0:left, :] = kp_ref[...]
    kw_ref[left:, :] = kc_ref[...]
    vw_ref[0:left, :] = vp_ref[...]
    vw_ref[left:, :] = vc_ref[...]
    lane = lax.broadcasted_iota(I32, (CHUNK, LANES), 1)
    m0 = lane < HEAD_DIM
    first_block = pl.program_id(1) == 0

    def chunk(i, carry, masked):
        r0 = pl.multiple_of(i * CHUNK, CHUNK)
        if masked:
            valid = lax.broadcasted_iota(I32, (CHUNK, BAND), 1) + r0 >= left
        for p in range(N_PAIRS):
            cols = slice(p * LANES, (p + 1) * LANES)
            q = q_ref[pl.ds(r0, CHUNK), cols]
            k = kw_ref[pl.ds(r0, BAND), cols]
            for hh in range(2):
                qm = jnp.where(m0 if hh == 0 else jnp.logical_not(m0), q, jnp.zeros_like(q))
                s = _dot_nt(qm, k) + bias_ref[2 * p + hh]
                if masked:
                    s = jnp.where(valid, s, -1e30)
                s_ref[2 * p + hh] = s
        for h in range(N_HEADS):
            s = s_ref[h]
            e = jnp.exp(s - jnp.max(s, axis=-1, keepdims=True))
            p_ref[h] = (e / jnp.sum(e, axis=-1, keepdims=True)).astype(BF16)
        pairs = []
        for p in range(N_PAIRS):
            v = vw_ref[pl.ds(r0, BAND), p * LANES:(p + 1) * LANES]
            pairs.append(jnp.where(m0, _dot(p_ref[2 * p], v), _dot(p_ref[2 * p + 1], v)))
        o_ref[pl.ds(r0, CHUNK), :] = jnp.concatenate(pairs, axis=1).astype(o_ref.dtype)
        return carry

    if mask_first:
        @pl.when(first_block)
        def _():
            lax.fori_loop(0, cpb, functools.partial(chunk, masked=True), 0)

        @pl.when(jnp.logical_not(first_block))
        def _():
            lax.fori_loop(0, cpb, functools.partial(chunk, masked=False), 0)
    else:
        lax.fori_loop(0, cpb, functools.partial(chunk, masked=False), 0)


def _attention(q_src, k_prev_src, kv_cur_src, bias, *, batch, seq, cpb, prev_is_cache):
    left = LEFT_CHUNKS * CHUNK
    rows = cpb * CHUNK
    nblk = seq // rows
    if prev_is_cache:
        prev_idx = lambda col: (lambda b, j: (b, col))
        kcol, vcol = 0, 1
    else:
        assert rows == left
        prev_idx = lambda col: (lambda b, j: (b * nblk + jnp.maximum(j - 1, 0), col))
        kcol, vcol = 1, 2
    cur = lambda col: (lambda b, j: (b * nblk + j, col))
    kern = functools.partial(_attn_kernel, cpb=cpb, mask_first=not prev_is_cache)
    return pl.pallas_call(
        kern,
        grid=(batch, nblk),
        in_specs=[
            pl.BlockSpec((rows, WIDTH), cur(0)),
            pl.BlockSpec((left, WIDTH), prev_idx(kcol)),
            pl.BlockSpec((rows, WIDTH), cur(1)),
            pl.BlockSpec((left, WIDTH), prev_idx(vcol)),
            pl.BlockSpec((rows, WIDTH), cur(2)),
            pl.BlockSpec(bias.shape, lambda b, j: (0, 0, 0)),
        ],
        out_specs=pl.BlockSpec((rows, WIDTH), lambda b, j: (b * nblk + j, 0)),
        out_shape=jax.ShapeDtypeStruct((batch * seq, WIDTH), BF16),
        scratch_shapes=[pltpu.VMEM((left + rows, WIDTH), BF16), pltpu.VMEM((left + rows, WIDTH), BF16),
                        pltpu.VMEM((N_HEADS, CHUNK, BAND), F32), pltpu.VMEM((N_HEADS, CHUNK, BAND), BF16)],
        compiler_params=_params(("parallel", "arbitrary")),
        name="attn",
    )(q_src, k_prev_src, kv_cur_src, k_prev_src, kv_cur_src, bias)


def _blockdiag(b, m0):
    z = jnp.zeros_like(b)
    return jnp.concatenate([jnp.where(m0, b, z), jnp.where(m0, z, b)], axis=0).astype(BF16)


def _pairdot(a, b, m0):
    return _dot(a.astype(BF16), _blockdiag(b, m0))


def _gdn_kernel(xbc_ref, gb_ref, ab_ref, convw_ref, alog_ref, dtb_ref, gnorm_ref, s0_ref, tail0_ref,
                bd_ref, eb_ref, eg_ref, lt_ref, eye2_ref,
                ob_ref, sout_ref,
                s_scr, tail_scr, xc_scr, kdt_scr, u_scr, q_scr, k_scr, v_scr, beta_scr, gc_scr, o_scr, pw_scr, t_scr, aqk_scr,
                *, rows):
    nchunks = rows // CHUNK

    ri = lax.broadcasted_iota(I32, (CHUNK, LANES), 0)
    ci = lax.broadcasted_iota(I32, (CHUNK, LANES), 1)
    m0 = ci < HEAD_DIM

    @pl.when(pl.program_id(1) == 0)
    def _():
        for p in range(N_PAIRS):
            s = s0_ref[0, p]
            z = jnp.zeros_like(s)
            s_scr[p] = jnp.concatenate([jnp.where(m0, s, z), jnp.where(m0, z, s)], axis=0)
        tail_scr[...] = tail0_ref[0]

    y = _conv_silu(xbc_ref[...], convw_ref[...], tail_scr, xc_scr)

    bd = bd_ref[...]
    q = y[:, 0:WIDTH]
    k = y[:, WIDTH:2 * WIDTH]
    q_scr[...] = q * lax.rsqrt(_group_sum(q * q, bd) + RMS_EPS) * (HEAD_DIM ** -0.5)
    k_scr[...] = k * lax.rsqrt(_group_sum(k * k, bd) + RMS_EPS)
    v_scr[...] = y[:, 2 * WIDTH:3 * WIDTH]

    ab = ab_ref[...]
    z = ab + dtb_ref[...]
    softplus = jnp.maximum(z, 0.0) + jnp.log1p(jnp.exp(-jnp.abs(z)))
    g = -jnp.exp(alog_ref[...]) * softplus
    beta_scr[...] = _dot3_rhs01(_sigmoid(ab), eb_ref[...])
    gexp = _dot3_rhs01(g, eg_ref[...])
    lt = lt_ref[...]
    for c in range(nchunks):
        gc_scr[c * CHUNK:(c + 1) * CHUNK, :] = _dot3_lhs01(lt, gexp[c * CHUNK:(c + 1) * CHUNK])

    cj = jnp.bitwise_and(ci, HEAD_DIM - 1)
    incl = ri >= cj
    strict = ri > cj
    eye2 = eye2_ref[...]
    ones = jnp.ones((CHUNK, CHUNK), BF16)
    r2 = lax.broadcasted_iota(I32, (LANES, LANES), 0)
    c2 = lax.broadcasted_iota(I32, (LANES, LANES), 1)
    eye128 = jnp.where(r2 == c2, 1.0, 0.0).astype(BF16)
    same_head = (r2 < HEAD_DIM) == (c2 < HEAD_DIM)

    def tiles(body, unroll):
        def step(c, carry):
            if isinstance(c, int):
                rs = slice(c * CHUNK, (c + 1) * CHUNK)
            else:
                rs = pl.ds(pl.multiple_of(c * CHUNK, CHUNK), CHUNK)
            for p in range(N_PAIRS):
                body(rs, slice(p * LANES, (p + 1) * LANES), c * N_PAIRS + p, p)
            return carry
        if nchunks <= unroll:
            for c in range(nchunks):
                step(c, 0)
        else:
            lax.fori_loop(0, nchunks, step, 0, unroll=unroll)

    def setup(rs, cols, tile, p):
        qp = q_scr[rs, cols]
        kp = k_scr[rs, cols]
        beta = beta_scr[rs, cols]
        gc = gc_scr[rs, cols]
        gl = gc[CHUNK - 1:CHUNK, :]
        eg = jnp.exp(gc)
        kb = kp * beta
        bdk = _blockdiag(kp, m0)
        kk = _dot_nt(kb.astype(BF16), bdk)
        qk = _dot_nt(qp.astype(BF16), bdk)
        gcol = _dot3_lhs01(ones, gc * eye2)
        decay = jnp.where(incl, jnp.exp(jnp.where(incl, gc - gcol, 0.0)), 0.0)
        nmat = -(kk * jnp.where(strict, decay, 0.0))
        pw_scr[rs, cols] = nmat
        t_scr[rs, cols] = eye2 + nmat
        aqk_scr[rs, cols] = qk * decay
        q_scr[rs, cols] = qp * eg
        kd = (kp * jnp.exp(gl - gc)).astype(BF16)
        kdt_scr[tile] = _dot_nt(eye128, kd).astype(BF16)
        v_scr[rs, cols] = v_scr[rs, cols] * beta
        beta_scr[rs, cols] = kb * eg

    tiles(setup, 8)

    def double(rs, cols, tile, p):
        pw = pw_scr[rs, cols]
        pw = _pairdot(pw, pw, m0)
        pw_scr[rs, cols] = pw
        t = t_scr[rs, cols]
        t_scr[rs, cols] = t + _pairdot(t, pw, m0)

    for _ in range(5):
        tiles(double, 8)

    def solve(rs, cols, tile, p):
        t = t_scr[rs, cols]
        v_scr[rs, cols] = _pairdot(t, v_scr[rs, cols], m0)
        beta_scr[rs, cols] = _pairdot(t, beta_scr[rs, cols], m0)

    tiles(solve, 8)

    def scan_u(rs, cols, tile, p):
        u_scr[p] = v_scr[rs, cols] - _dot(beta_scr[rs, cols].astype(BF16), s_scr[p].astype(BF16))

    def scan_s(rs, cols, tile, p):
        s = s_scr[p]
        u = u_scr[p]
        o_scr[rs, cols] = (_dot(q_scr[rs, cols].astype(BF16), s.astype(BF16))
                           + _pairdot(aqk_scr[rs, cols], u, m0))
        f = _dot(kdt_scr[tile], u.astype(BF16))
        gt = jnp.exp(gc_scr[rs, cols][CHUNK - 1:CHUNK, :])
        s_scr[p] = s * gt + jnp.where(same_head, f, 0.0)

    def scan(c, carry):
        rs = pl.ds(pl.multiple_of(c * CHUNK, CHUNK), CHUNK)
        for body in (scan_u, scan_s):
            for p in range(N_PAIRS):
                body(rs, slice(p * LANES, (p + 1) * LANES), c * N_PAIRS + p, p)
        return carry

    lax.fori_loop(0, nchunks, scan, 0)

    o = o_scr[...]
    on = o * lax.rsqrt(_group_sum(o * o, bd) * (1.0 / HEAD_DIM) + RMS_EPS) * gnorm_ref[...]
    gate = gb_ref[...]
    ob_ref[...] = (on * (gate * _sigmoid(gate))).astype(ob_ref.dtype)
    for p in range(N_PAIRS):
        s = s_scr[p]
        sout_ref[0, p] = jnp.where(m0, s[0:HEAD_DIM], s[HEAD_DIM:2 * HEAD_DIM])


def _gdn(xbc, gb, ab, convw, alog, dtb, gnorm, s0, tail0, consts, *, batch, seq, rows):
    nblk = seq // rows
    bd, eb, eg, lt, eye2 = consts
    blk = lambda b, j: (b * nblk + j, 0)
    c2 = lambda b, j: (0, 0)
    per_b3 = lambda b, j: (b, 0, 0)
    per_b4 = lambda b, j: (b, 0, 0, 0)
    kern = functools.partial(_gdn_kernel, rows=rows)
    wide = pltpu.VMEM((rows, WIDTH), F32)
    return pl.pallas_call(
        kern,
        grid=(batch, nblk),
        in_specs=[
            pl.BlockSpec((rows, 3 * WIDTH), blk),
            pl.BlockSpec((rows, WIDTH), blk),
            pl.BlockSpec((rows, LANES), blk),
            pl.BlockSpec(convw.shape, c2),
            pl.BlockSpec(alog.shape, c2),
            pl.BlockSpec(dtb.shape, c2),
            pl.BlockSpec(gnorm.shape, c2),
            pl.BlockSpec((1, N_PAIRS, HEAD_DIM, LANES), per_b4),
            pl.BlockSpec((1, CONV_TAIL, 3 * WIDTH), per_b3),
            pl.BlockSpec(bd.shape, c2),
            pl.BlockSpec(eb.shape, c2),
            pl.BlockSpec(eg.shape, c2),
            pl.BlockSpec(lt.shape, c2),
            pl.BlockSpec(eye2.shape, c2),
        ],
        out_specs=[
            pl.BlockSpec((rows, WIDTH), blk),
            pl.BlockSpec((1, N_PAIRS, HEAD_DIM, LANES), per_b4),
        ],
        out_shape=[
            jax.ShapeDtypeStruct((batch * seq, WIDTH), BF16),
            jax.ShapeDtypeStruct((batch, N_PAIRS, HEAD_DIM, LANES), F32),
        ],
        scratch_shapes=[
            pltpu.VMEM((N_PAIRS, LANES, LANES), F32),
            pltpu.VMEM((CONV_TAIL, 3 * WIDTH), F32),
            pltpu.VMEM((CONV_TAIL + rows, 3 * WIDTH), F32),
            pltpu.VMEM((rows // CHUNK * N_PAIRS, LANES, HEAD_DIM), BF16),
            pltpu.VMEM((N_PAIRS, CHUNK, LANES), F32),
            wide, wide, wide, wide, wide, wide, wide, wide, wide,
        ],
        compiler_params=_params(("parallel", "arbitrary")),
        name="gdn",
    )(xbc, gb, ab, convw, alog, dtb, gnorm, s0, tail0, bd, eb, eg, lt, eye2)


def _post_kernel(oa_ref, ob_ref, gate_ref, x_ref, wa_ref, wb_ref, wo_ref, nf_ref, wr_ref,
                 x1_ref, h2_ref, route_ref):
    d = x_ref.shape[1]
    a = _dot(oa_ref[...], wa_ref[...])
    b = _dot(ob_ref[...], wb_ref[...])
    merged = _sigmoid(gate_ref[:, 0:d]) * a + _sigmoid(gate_ref[:, d:2 * d]) * b
    x1 = x_ref[...] + _dot(merged.astype(BF16), wo_ref[...])
    x1_ref[...] = x1
    h2 = x1 * lax.rsqrt(jnp.mean(x1 * x1, axis=-1, keepdims=True) + RMS_EPS) * nf_ref[...]
    h2_ref[...] = h2

    hi, lo = _split2(h2)
    wr = wr_ref[...]
    hw = _dot(hi, wr)
    logits = hw[:, 0:LANES] + (_dot(lo, wr[:, 0:LANES]) + hw[:, LANES:])

    lane = lax.broadcasted_iota(I32, logits.shape, 1).astype(F32)
    big = float(LANES)
    ninf = -jnp.inf
    lg = jnp.where(lane < N_GROUPS, logits, ninf)
    mg = jnp.max(lg, axis=-1, keepdims=True)
    grp = jnp.min(jnp.where(lg == mg, lane, big), axis=-1, keepdims=True)
    p_grp = 1.0 / jnp.sum(jnp.exp(lg - mg), axis=-1, keepdims=True)
    lo_lane = N_GROUPS + grp * EXPERTS_PER_GROUP
    le = jnp.where(jnp.logical_and(lane >= lo_lane, lane < lo_lane + EXPERTS_PER_GROUP), logits, ninf)
    m1 = jnp.max(le, axis=-1, keepdims=True)
    i1 = jnp.min(jnp.where(le == m1, lane, big), axis=-1, keepdims=True)
    le2 = jnp.where(lane == i1, ninf, le)
    m2 = jnp.max(le2, axis=-1, keepdims=True)
    i2 = jnp.min(jnp.where(le2 == m2, lane, big), axis=-1, keepdims=True)
    e2 = jnp.exp(m2 - m1)
    w1 = 1.0 / (1.0 + e2) * p_grp
    w2 = e2 / (1.0 + e2) * p_grp
    out = jnp.where(lane == 0, i1 - N_GROUPS, 0.0)
    out = jnp.where(lane == 1, i2 - N_GROUPS, out)
    out = jnp.where(lane == 2, w1, out)
    out = jnp.where(lane == 3, w2, out)
    route_ref[...] = out


def _post(oa, ob, gate, x, wa, wb, wo, nf, wr, tm):
    n, d = x.shape
    row = lambda i: (i, 0)
    const = lambda i: (0, 0)
    return pl.pallas_call(
        _post_kernel,
        grid=(n // tm,),
        in_specs=[
            pl.BlockSpec((tm, WIDTH), row), pl.BlockSpec((tm, WIDTH), row),
            pl.BlockSpec((tm, 2 * d), row), pl.BlockSpec((tm, d), row),
            pl.BlockSpec(wa.shape, const), pl.BlockSpec(wb.shape, const), pl.BlockSpec(wo.shape, const),
            pl.BlockSpec(nf.shape, const), pl.BlockSpec(wr.shape, const),
        ],
        out_specs=[pl.BlockSpec((tm, d), row), pl.BlockSpec((tm, d), row), pl.BlockSpec((tm, LANES), row)],
        out_shape=[
            jax.ShapeDtypeStruct((n, d), F32),
            jax.ShapeDtypeStruct((n, d), F32),
            jax.ShapeDtypeStruct((n, LANES), F32),
        ],
        compiler_params=_params(("parallel",)),
        name="post",
    )(oa, ob, gate, x, wa, wb, wo, nf, wr)


def _rank_kernel(route_ref, tril_ref, rank_ref, cnt_ref, carry_scr):
    @pl.when(pl.program_id(0) == 0)
    def _():
        carry_scr[...] = jnp.zeros_like(carry_scr)

    r = route_ref[...]
    lane = lax.broadcasted_iota(I32, r.shape, 1)
    lanef = lane.astype(F32)
    oh1 = lanef == r[:, 0:1]
    oh2 = lanef == r[:, 1:2]
    oh = jnp.where(jnp.logical_or(oh1, oh2), 1.0, 0.0)
    before = _dot(tril_ref[...], oh.astype(BF16)) + carry_scr[...]
    rank1 = jnp.sum(jnp.where(oh1, before, 0.0), axis=-1, keepdims=True)
    rank2 = jnp.sum(jnp.where(oh2, before, 0.0), axis=-1, keepdims=True)
    rank_ref[...] = jnp.where(lane == 0, rank1, jnp.where(lane == 1, rank2, 0.0))
    carry_scr[...] = carry_scr[...] + jnp.sum(oh, axis=0, keepdims=True)
    cnt_ref[...] = carry_scr[...]


def _rank(route, tril):
    n = route.shape[0]
    tr = min(tril.shape[0], n)
    tril = tril[:tr, :tr]
    return pl.pallas_call(
        _rank_kernel,
        grid=(n // tr,),
        in_specs=[pl.BlockSpec((tr, LANES), lambda i: (i, 0)), pl.BlockSpec(tril.shape, lambda i: (0, 0))],
        out_specs=[pl.BlockSpec((tr, LANES), lambda i: (i, 0)), pl.BlockSpec((1, LANES), lambda i: (0, 0))],
        out_shape=[jax.ShapeDtypeStruct((n, LANES), F32), jax.ShapeDtypeStruct((1, LANES), F32)],
        scratch_shapes=[pltpu.VMEM((1, LANES), F32)],
        compiler_params=_params(("arbitrary",)),
        name="rank",
    )(route, tril)


def _dest_kernel(route_ref, rank_ref, pstart_ref, dest_ref):
    r = route_ref[...]
    rk = rank_ref[...]
    ps = pstart_ref[...]
    lane = lax.broadcasted_iota(I32, r.shape, 1)
    lanef = lane.astype(F32)
    d1 = jnp.sum(jnp.where(lanef == r[:, 0:1], ps, 0.0), axis=-1, keepdims=True) + rk[:, 0:1]
    d2 = jnp.sum(jnp.where(lanef == r[:, 1:2], ps, 0.0), axis=-1, keepdims=True) + rk[:, 1:2]
    dest_ref[...] = jnp.where(lane == 0, d1, jnp.where(lane == 1, d2, 0.0)).astype(I32)


def _dest(route, rank, pstart, tr):
    n = route.shape[0]
    row = lambda i: (i, 0)
    return pl.pallas_call(
        _dest_kernel,
        grid=(n // tr,),
        in_specs=[pl.BlockSpec((tr, LANES), row), pl.BlockSpec((tr, LANES), row),
                  pl.BlockSpec((1, LANES), lambda i: (0, 0))],
        out_specs=pl.BlockSpec((tr, LANES), row),
        out_shape=jax.ShapeDtypeStruct((n, LANES), I32),
        compiler_params=_params(("parallel",)),
        name="dest",
    )(route, rank, pstart)


def _row_copy(src_ref, src_row, dst_ref, dst_row, sem):
    return pltpu.make_async_copy(src_ref.at[pl.ds(src_row, 1)], dst_ref.at[pl.ds(dst_row, 1)], sem)


def _for_rows(td, per_row):
    def group(i, carry):
        for j in range(ROW_UNROLL):
            per_row(i * ROW_UNROLL + j, j)
        return carry
    lax.fori_loop(0, td // ROW_UNROLL, group, 0)


def _dispatch_kernel(zblk_ref, nused_ref, dest_ref, h_ref, xs_ref, zero_ref, sem, zsem, *, nb):
    td = h_ref.shape[0]
    rows = zero_ref.shape[0]

    @pl.when(pl.program_id(0) == 0)
    def _():
        zero_ref[...] = jnp.zeros_like(zero_ref)

        def zero_copy(blk):
            return pltpu.make_async_copy(zero_ref, xs_ref.at[pl.ds(pl.multiple_of(blk * rows, rows), rows)], zsem)

        def for_zero_blocks(act):
            for e in range(N_EXPERTS):
                @pl.when(zblk_ref[e] >= 0)
                def _():
                    act(zero_copy(zblk_ref[e]))

            def tail(blk, carry):
                act(zero_copy(blk))
                return carry
            lax.fori_loop(nused_ref[0], nb, tail, 0)

        for_zero_blocks(lambda cp: cp.start())
        for_zero_blocks(lambda cp: cp.wait())

    def issue(t, j):
        _row_copy(h_ref, t, xs_ref, dest_ref[0, 0, 2 * t], sem).start(priority=0)
        _row_copy(h_ref, t, xs_ref, dest_ref[0, 0, 2 * t + 1], sem).start(priority=1)

    _for_rows(td, issue)
    whole = pltpu.make_async_copy(h_ref, xs_ref.at[pl.ds(0, td)], sem)
    whole.wait()
    whole.wait()


def _dispatch(zblk, nused, dest, h2, td, rows, nb):
    n, d = h2.shape
    grid_spec = pltpu.PrefetchScalarGridSpec(
        num_scalar_prefetch=2,
        grid=(n // td,),
        in_specs=[
            pl.BlockSpec((1, 1, 2 * td), lambda i, zb, nu: (i, 0, 0), memory_space=pltpu.SMEM),
            pl.BlockSpec((td, d), lambda i, zb, nu: (i, 0)),
        ],
        out_specs=pl.BlockSpec(memory_space=pl.ANY),
        scratch_shapes=[pltpu.VMEM((rows, d), F32), pltpu.SemaphoreType.DMA(()), pltpu.SemaphoreType.DMA(())],
    )
    return pl.pallas_call(
        functools.partial(_dispatch_kernel, nb=nb),
        grid_spec=grid_spec,
        out_shape=jax.ShapeDtypeStruct((nb * rows, d), F32),
        compiler_params=_params(("arbitrary",)),
        name="dispatch",
    )(zblk, nused, dest, h2)


def _ffn_kernel(blk_e_ref, nused_ref, xs_ref, wg_ref, wu_ref, wd_ref, ys_ref):
    del blk_e_ref
    i = pl.program_id(0)

    @pl.when(i < nused_ref[0])
    def _():
        xb = xs_ref[...].astype(BF16)
        g = _dot(xb, wg_ref[0].astype(BF16))
        u = _dot(xb, wu_ref[0].astype(BF16))
        hb = (g * _sigmoid(g)) * u
        ys_ref[...] = _dot(hb.astype(BF16), wd_ref[0].astype(BF16))

    @pl.when(i >= nused_ref[0])
    def _():
        ys_ref[...] = jnp.zeros_like(ys_ref)


def _ffn(blk_e, nused, xs, wg, wu, wd, rows):
    npad, d = xs.shape
    f = wg.shape[2]
    nb = npad // rows
    grid_spec = pltpu.PrefetchScalarGridSpec(
        num_scalar_prefetch=2,
        grid=(nb,),
        in_specs=[
            pl.BlockSpec((rows, d), lambda i, be, nu: (i, 0)),
            pl.BlockSpec((1, d, f), lambda i, be, nu: (be[i], 0, 0)),
            pl.BlockSpec((1, d, f), lambda i, be, nu: (be[i], 0, 0)),
            pl.BlockSpec((1, f, d), lambda i, be, nu: (be[i], 0, 0)),
        ],
        out_specs=pl.BlockSpec((rows, d), lambda i, be, nu: (i, 0)),
    )
    return pl.pallas_call(
        _ffn_kernel,
        grid_spec=grid_spec,
        out_shape=jax.ShapeDtypeStruct((npad, d), F32),
        compiler_params=_params(("arbitrary",)),
        name="ffn",
    )(blk_e, nused, xs, wg, wu, wd)


def _combine_kernel(dest_ref, x1_ref, route_ref, nfin_ref, ys_ref, out_ref, g1_ref, g2_ref, sem):
    td = x1_ref.shape[0]

    def issue(t, j):
        _row_copy(ys_ref, dest_ref[0, 0, 2 * t], g1_ref, t, sem).start(priority=0)
        _row_copy(ys_ref, dest_ref[0, 0, 2 * t + 1], g2_ref, t, sem).start(priority=1)

    _for_rows(td, issue)
    pltpu.make_async_copy(ys_ref.at[pl.ds(0, td)], g1_ref, sem).wait()
    pltpu.make_async_copy(ys_ref.at[pl.ds(0, td)], g2_ref, sem).wait()

    r = route_ref[...]
    y = x1_ref[...] + (r[:, 2:3] * g1_ref[...] + r[:, 3:4] * g2_ref[...])
    out_ref[...] = y * lax.rsqrt(jnp.mean(y * y, axis=-1, keepdims=True) + RMS_EPS) * nfin_ref[...]


def _combine(dest, x1, route, nfin, ys, td):
    n, d = x1.shape
    return pl.pallas_call(
        _combine_kernel,
        grid=(n // td,),
        in_specs=[
            pl.BlockSpec((1, 1, 2 * td), lambda i: (i, 0, 0), memory_space=pltpu.SMEM),
            pl.BlockSpec((td, d), lambda i: (i, 0)),
            pl.BlockSpec((td, LANES), lambda i: (i, 0)),
            pl.BlockSpec((1, d), lambda i: (0, 0)),
            pl.BlockSpec(memory_space=pl.ANY),
        ],
        out_specs=pl.BlockSpec((td, d), lambda i: (i, 0)),
        out_shape=jax.ShapeDtypeStruct((n, d), F32),
        scratch_shapes=[pltpu.VMEM((td, d), F32), pltpu.VMEM((td, d), F32), pltpu.SemaphoreType.DMA(())],
        compiler_params=_params(("arbitrary",)),
        name="combine",
    )(dest, x1, route, nfin, ys)


def _moe(x1, h2, route, wg, wu, wd, nfin, tril, td):
    n, d = x1.shape
    rank, cnt = _rank(route, tril)
    rows = MOE_ROWS if 2 * n >= N_EXPERTS * MOE_ROWS else MOE_ROWS_SMALL
    counts = cnt[0, :N_EXPERTS].astype(I32)
    pcounts = (counts + rows - 1) // rows * rows
    pends = jnp.cumsum(pcounts)
    pstarts = pends - pcounts
    nb = (2 * n) // rows + N_EXPERTS
    blk_e = jnp.minimum(
        jnp.sum(pends[None, :] <= (jnp.arange(nb, dtype=I32) * rows)[:, None], axis=1), N_EXPERTS - 1
    ).astype(I32)
    nused = (pends[-1:] // rows).astype(I32)
    pstart_row = jnp.pad(pstarts.astype(F32), (0, LANES - N_EXPERTS))[None, :]
    dest = _dest(route, rank, pstart_row, min(n, 1024))[:, 0:2].reshape(n // td, 1, 2 * td)
    zblk = jnp.where(pcounts > 0, pends // rows - 1, -1).astype(I32)
    xs = _dispatch(zblk, nused, dest, h2, td, rows, nb)
    ys = _ffn(blk_e, nused, xs, wg, wu, wd, rows)
    return _combine(dest, x1, route, nfin, ys, td)


def _layer(x, past_k, past_v, s0, conv_buf, wts, consts, *, prompt):
    (norm_mix, w_in_parts, bias, conv_w, alog, dtb, gnorm, wa, wb, wo, norm_ffn, wr,
     wg, wu, wd, norm_final) = wts
    batch, seq, d = x.shape
    n = batch * seq
    left = LEFT_CHUNKS * CHUNK
    tm = 512 if n % 512 == 0 else n
    xf = x.reshape(n, d)
    keep = min(left, seq)
    assert (seq % tm == 0 and tm == keep) or (tm % seq == 0 and keep == seq)
    tiles_per_seq = max(seq // tm, 1)
    qkv, kv, xbc, gb, ab, gate = _inproj(xf, norm_mix, w_in_parts, tm, tiles_per_seq)

    if prompt:
        o_a = _attention(qkv, qkv, qkv, bias, batch=batch, seq=seq, cpb=LEFT_CHUNKS, prev_is_cache=False)
        rows = left
        s_init = jnp.zeros((batch, N_PAIRS, HEAD_DIM, LANES), F32)
        tail = jnp.zeros((batch, CONV_TAIL, 3 * WIDTH), F32)
    else:
        cache = jnp.concatenate(
            [past_k.reshape(batch * left, WIDTH), past_v.reshape(batch * left, WIDTH)], axis=1).astype(BF16)
        o_a = _attention(qkv, cache, qkv, bias, batch=batch, seq=seq, cpb=seq // CHUNK, prev_is_cache=True)
        rows = seq
        s_init = s0.reshape(batch, N_PAIRS, 2, HEAD_DIM, HEAD_DIM).transpose(0, 1, 3, 2, 4).reshape(
            batch, N_PAIRS, HEAD_DIM, LANES)
        tail = jnp.pad(conv_buf, ((0, 0), (CONV_TAIL - (CONV_W - 1), 0), (0, 0)))

    bd, eb, eg, lt, eye2, tril = consts
    o_b, s_new = _gdn(xbc, gb, ab, conv_w, alog, dtb, gnorm, s_init, tail,
                      (bd, eb, eg, lt, eye2), batch=batch, seq=seq, rows=rows)

    x1, h2, route = _post(o_a, o_b, gate, xf, wa, wb, wo, norm_ffn, wr, tm)
    y = _moe(x1, h2, route, wg, wu, wd, norm_final, tril, td=min(1024, n))

    kv4 = kv.reshape(batch, keep, 2, N_HEADS, HEAD_DIM)
    new_k = kv4[:, :, 0]
    new_v = kv4[:, :, 1]
    new_s = s_new.reshape(batch, N_PAIRS, HEAD_DIM, 2, HEAD_DIM).transpose(0, 1, 3, 2, 4).reshape(
        batch, N_HEADS, HEAD_DIM, HEAD_DIM)
    new_conv = xbc.reshape(batch, seq, 3 * WIDTH)[:, seq - (CONV_W - 1):]
    return y.reshape(batch, seq, d), new_k, new_v, new_s, new_conv


def _constants():
    lane = jnp.arange(WIDTH)
    src = jnp.arange(LANES)
    bd = (src[:, None] // HEAD_DIM == src[None, :] // HEAD_DIM).astype(BF16)
    eb = (src[:, None] == lane[None, :] // HEAD_DIM).astype(BF16)
    eg = (src[:, None] == N_HEADS + lane[None, :] // HEAD_DIM).astype(BF16)
    c = jnp.arange(CHUNK)
    lt = (c[:, None] >= c[None, :]).astype(BF16)
    eye2 = (c[:, None] == jnp.arange(LANES)[None, :] % HEAD_DIM).astype(F32)
    t = jnp.arange(512)
    tril = (t[:, None] > t[None, :]).astype(BF16)
    return bd, eb, eg, lt, eye2, tril


def kernel(x_prompt, x_sample, cache_attn_k, cache_attn_v, state_delta, state_conv, norm_mix, w_in, rel_bias,
           conv_w, a_log, dt_bias, gdn_norm, w_branch_a, w_branch_b, w_out, norm_ffn, w_route_group,
           w_route_expert, w_gate, w_up, w_down, norm_final):
    depth = w_in.shape[0]
    d = x_prompt.shape[-1]
    left = LEFT_CHUNKS * CHUNK
    consts = _constants()

    def bias_table(rb):
        nd = BAND + CHUNK - 1
        d_rev = (left + CHUNK - 1) - jnp.arange(nd)
        vr = rb[:, jnp.clip(d_rev, -REL_CLIP, REL_CLIP) + REL_CLIP]
        return jnp.stack([vr[:, CHUNK - 1 - i:CHUNK - 1 - i + BAND] for i in range(CHUNK)], axis=1)

    xp, xs = x_prompt, x_sample
    outs_p, outs_s = [], []
    for l in range(depth):
        w = w_in[l]
        o = 0
        parts = []
        for width in (3 * WIDTH, 3 * WIDTH, WIDTH, 2 * N_HEADS, 2 * d):
            parts.append(w[:, o:o + width])
            o += width
        wqkv, wxbc, wgb, wab, wgate = parts
        wqkv = jnp.concatenate([wqkv[:, :WIDTH] * (HEAD_DIM ** -0.5), wqkv[:, WIDTH:]], axis=1)
        wab = jnp.pad(wab, ((0, 0), (0, LANES - 2 * N_HEADS)))
        w_in_parts = tuple(m.astype(BF16) for m in (wqkv, wxbc, wgb, wab, wgate))
        lane_pad = lambda v: jnp.pad(v[None, :], ((0, 0), (N_HEADS, LANES - 2 * N_HEADS)))
        wr = jnp.pad(jnp.concatenate([w_route_group[l], w_route_expert[l]], axis=1),
                     ((0, 0), (0, LANES - N_GROUPS - N_EXPERTS)))
        wrh = wr.astype(BF16)
        wrl = (wr - wrh.astype(F32)).astype(BF16)
        wts = (
            norm_mix[l][None, :], w_in_parts, bias_table(rel_bias[l]), conv_w[l], lane_pad(a_log[l]),
            lane_pad(dt_bias[l]), jnp.tile(gdn_norm[l], N_HEADS)[None, :],
            w_branch_a[l].astype(BF16), w_branch_b[l].astype(BF16), w_out[l].astype(BF16),
            norm_ffn[l][None, :], jnp.concatenate([wrh, wrl], axis=1),
            w_gate[l], w_up[l], w_down[l], norm_final[None, :],
        )
        assert depth == 1
        xp, kp, vp, sp, cp = _layer(xp, None, None, None, None, wts, consts, prompt=True)
        xs, ks, vs, ss, cs = _layer(xs, cache_attn_k[l], cache_attn_v[l], state_delta[l], state_conv[l],
                                    wts, consts, prompt=False)
        outs_p.append((kp, vp, sp, cp))
        outs_s.append((ks, vs, ss, cs))
    stack = lambda items, idx: jnp.stack([it[idx] for it in items])
    return (xp, xs,
            stack(outs_p, 0), stack(outs_p, 1), stack(outs_p, 2), stack(outs_p, 3),
            stack(outs_s, 0), stack(outs_s, 1), stack(outs_s, 2), stack(outs_s, 3))
```

```python
import functools

import jax
import jax.numpy as jnp
from jax import lax
from jax.experimental import pallas as pl
from jax.experimental.pallas import tpu as pltpu

F32 = jnp.float32
BF16 = jnp.bfloat16
I32 = jnp.int32

RMS_EPS = 1e-6
CHUNK = 64
LEFT_CHUNKS = 8
BAND = (LEFT_CHUNKS + 1) * CHUNK
N_HEADS = 8
HEAD_DIM = 64
N_PAIRS = N_HEADS // 2
WIDTH = N_HEADS * HEAD_DIM
REL_CLIP = 128
CONV_W = 4
N_GROUPS = 4
EXPERTS_PER_GROUP = 8
N_EXPERTS = N_GROUPS * EXPERTS_PER_GROUP
LANES = 128
CONV_TAIL = 8
MOE_ROWS = 512
MOE_ROWS_SMALL = 128
ROW_UNROLL = 8
COMBINE_GROUP = 64
VMEM_LIMIT = 56 * 1024 * 1024


def _dot(a, b):
    return jnp.dot(a, b, preferred_element_type=F32)


def _dot_nt(a, b):
    return lax.dot_general(a, b, (((1,), (1,)), ((), ())), preferred_element_type=F32)


def _dot_tn(a, b):
    return lax.dot_general(a, b, (((0,), (0,)), ((), ())), preferred_element_type=F32)


def _sigmoid(x):
    return 1.0 / (1.0 + jnp.exp(-x))


def _split2(x):
    hi = x.astype(BF16)
    lo = (x - hi.astype(F32)).astype(BF16)
    return hi, lo


def _split3(x):
    hi = x.astype(BF16)
    r = x - hi.astype(F32)
    mid = r.astype(BF16)
    lo = (r - mid.astype(F32)).astype(BF16)
    return hi, mid, lo


def _dot3_rhs01(x, mat):
    hi, mid, lo = _split3(x)
    return (_dot(hi, mat) + _dot(mid, mat)) + _dot(lo, mat)


def _dot3_lhs01(mat, x):
    hi, mid, lo = _split3(x)
    return (_dot(mat, hi) + _dot(mat, mid)) + _dot(mat, lo)


def _group_sum(x2, bd):
    hi, lo = _split2(x2)
    outs = []
    for p in range(x2.shape[1] // LANES):
        cols = slice(p * LANES, (p + 1) * LANES)
        outs.append(_dot(hi[:, cols], bd) + _dot(lo[:, cols], bd))
    return jnp.concatenate(outs, axis=1)


def _params(sem):
    return pltpu.CompilerParams(dimension_semantics=sem, vmem_limit_bytes=VMEM_LIMIT)


def _conv_silu(x, w, tail_scr, xc_scr):
    rows = x.shape[0]
    xc_scr[0:CONV_TAIL, :] = tail_scr[...]
    xc_scr[CONV_TAIL:, :] = x
    y = xc_scr[CONV_TAIL - 3:CONV_TAIL - 3 + rows, :] * w[0:1]
    y = y + xc_scr[CONV_TAIL - 2:CONV_TAIL - 2 + rows, :] * w[1:2]
    y = y + xc_scr[CONV_TAIL - 1:CONV_TAIL - 1 + rows, :] * w[2:3]
    y = y + x * w[3:4]
    tail_scr[...] = x[rows - CONV_TAIL:rows]
    return y * _sigmoid(y)


def _inproj_kernel(x_ref, g_ref, wqkv_ref, wxbc_ref, wgb_ref, wab_ref, wgate_ref,
                   qkv_ref, kv_ref, xbc_ref, gb_ref, ab_ref, gate_ref, *, tiles_per_seq):
    x = x_ref[...]
    h = x * lax.rsqrt(jnp.mean(x * x, axis=-1, keepdims=True) + RMS_EPS) * g_ref[...]
    hb = h.astype(BF16)
    qkv_ref[...] = _dot(hb, wqkv_ref[...]).astype(BF16)

    @pl.when(pl.program_id(0) % tiles_per_seq == tiles_per_seq - 1)
    def _():
        kv_ref[...] = _dot(hb, wqkv_ref[:, WIDTH:])

    xbc_ref[...] = _dot(hb, wxbc_ref[...])
    gb_ref[...] = _dot(hb, wgb_ref[...])
    ab_ref[...] = _dot(hb, wab_ref[...])
    gate_ref[...] = _dot(hb, wgate_ref[...])


def _inproj(x, g, w, tm, tiles_per_seq):
    n, d = x.shape
    wqkv, wxbc, wgb, wab, wgate = w
    row = lambda i: (i, 0)
    tail = lambda i: (i // tiles_per_seq, 0)
    const = lambda i: (0, 0)
    widths = (3 * WIDTH, 2 * WIDTH, 3 * WIDTH, WIDTH, LANES, 2 * d)
    dtypes = (BF16, F32, F32, F32, F32, F32)
    nrows = (n, n // tiles_per_seq, n, n, n, n)
    maps = (row, tail, row, row, row, row)
    return pl.pallas_call(
        functools.partial(_inproj_kernel, tiles_per_seq=tiles_per_seq),
        grid=(n // tm,),
        in_specs=[pl.BlockSpec((tm, d), row), pl.BlockSpec((1, d), const)]
        + [pl.BlockSpec(m.shape, const) for m in w],
        out_specs=[pl.BlockSpec((tm, c), m) for c, m in zip(widths, maps)],
        out_shape=[jax.ShapeDtypeStruct((r, c), t) for r, c, t in zip(nrows, widths, dtypes)],
        compiler_params=_params(("arbitrary",)),
        name="inproj",
    )(x, g, wqkv, wxbc, wgb, wab, wgate)


def _attn_kernel(q_ref, kp_ref, kc_ref, vp_ref, vc_ref, bias_ref, o_ref, kw_ref, vw_ref, s_ref, p_ref,
                 *, cpb, mask_first):
    left = LEFT_CHUNKS * CHUNK
    kw_ref[0:left, :] = kp_ref[...]
    kw_ref[left:, :] = kc_ref[...]
    vw_ref[0:left, :] = vp_ref[...]
    vw_ref[left:, :] = vc_ref[...]
    lane = lax.broadcasted_iota(I32, (CHUNK, LANES), 1)
    m0 = lane < HEAD_DIM
    first_block = pl.program_id(1) == 0

    def chunk(i, carry, masked):
        r0 = pl.multiple_of(i * CHUNK, CHUNK)
        if masked:
            valid = lax.broadcasted_iota(I32, (CHUNK, BAND), 1) + r0 >= left
        for p in range(N_PAIRS):
            cols = slice(p * LANES, (p + 1) * LANES)
            q = q_ref[pl.ds(r0, CHUNK), cols]
            k = kw_ref[pl.ds(r0, BAND), cols]
            for hh in range(2):
                qm = jnp.where(m0 if hh == 0 else jnp.logical_not(m0), q, jnp.zeros_like(q))
                s = _dot_nt(qm, k) + bias_ref[2 * p + hh]
                if masked:
                    s = jnp.where(valid, s, -1e30)
                s_ref[2 * p + hh] = s
        for h in range(N_HEADS):
            s = s_ref[h]
            e = jnp.exp(s - jnp.max(s, axis=-1, keepdims=True))
            p_ref[h] = (e / jnp.sum(e, axis=-1, keepdims=True)).astype(BF16)
        pairs = []
        for p in range(N_PAIRS):
            v = vw_ref[pl.ds(r0, BAND), p * LANES:(p + 1) * LANES]
            pairs.append(jnp.where(m0, _dot(p_ref[2 * p], v), _dot(p_ref[2 * p + 1], v)))
        o_ref[pl.ds(r0, CHUNK), :] = jnp.concatenate(pairs, axis=1).astype(o_ref.dtype)
        return carry

    if mask_first:
        @pl.when(first_block)
        def _():
            lax.fori_loop(0, cpb, functools.partial(chunk, masked=True), 0)

        @pl.when(jnp.logical_not(first_block))
        def _():
            lax.fori_loop(0, cpb, functools.partial(chunk, masked=False), 0)
    else:
        lax.fori_loop(0, cpb, functools.partial(chunk, masked=False), 0)


def _attention(q_src, k_prev_src, kv_cur_src, bias, *, batch, seq, cpb, prev_is_cache):
    left = LEFT_CHUNKS * CHUNK
    rows = cpb * CHUNK
    nblk = seq // rows
    if prev_is_cache:
        prev_idx = lambda col: (lambda b, j: (b, col))
        kcol, vcol = 0, 1
    else:
        assert rows == left
        prev_idx = lambda col: (lambda b, j: (b * nblk + jnp.maximum(j - 1, 0), col))
        kcol, vcol = 1, 2
    cur = lambda col: (lambda b, j: (b * nblk + j, col))
    kern = functools.partial(_attn_kernel, cpb=cpb, mask_first=not prev_is_cache)
    return pl.pallas_call(
        kern,
        grid=(batch, nblk),
        in_specs=[
            pl.BlockSpec((rows, WIDTH), cur(0)),
            pl.BlockSpec((left, WIDTH), prev_idx(kcol)),
            pl.BlockSpec((rows, WIDTH), cur(1)),
            pl.BlockSpec((left, WIDTH), prev_idx(vcol)),
            pl.BlockSpec((rows, WIDTH), cur(2)),
            pl.BlockSpec(bias.shape, lambda b, j: (0, 0, 0)),
        ],
        out_specs=pl.BlockSpec((rows, WIDTH), lambda b, j: (b * nblk + j, 0)),
        out_shape=jax.ShapeDtypeStruct((batch * seq, WIDTH), BF16),
        scratch_shapes=[pltpu.VMEM((left + rows, WIDTH), BF16), pltpu.VMEM((left + rows, WIDTH), BF16),
                        pltpu.VMEM((N_HEADS, CHUNK, BAND), F32), pltpu.VMEM((N_HEADS, CHUNK, BAND), BF16)],
        compiler_params=_params(("parallel", "arbitrary")),
        name="attn",
    )(q_src, k_prev_src, kv_cur_src, k_prev_src, kv_cur_src, bias)


def _blockdiag(b, m0):
    z = jnp.zeros_like(b)
    return jnp.concatenate([jnp.where(m0, b, z), jnp.where(m0, z, b)], axis=0).astype(BF16)


def _pairdot(a, b, m0):
    return _dot(a.astype(BF16), _blockdiag(b, m0))


def _gdn_kernel(xbc_ref, gb_ref, ab_ref, convw_ref, alog_ref, dtb_ref, gnorm_ref, s0_ref, tail0_ref,
                bd_ref, eb_ref, eg_ref, lt_ref, eye2_ref,
                ob_ref, sout_ref,
                s_scr, tail_scr, xc_scr, kdt_scr, u_scr, q_scr, k_scr, v_scr, beta_scr, gc_scr, o_scr, pw_scr, t_scr, aqk_scr,
                *, rows):
    nchunks = rows // CHUNK

    ri = lax.broadcasted_iota(I32, (CHUNK, LANES), 0)
    ci = lax.broadcasted_iota(I32, (CHUNK, LANES), 1)
    m0 = ci < HEAD_DIM

    @pl.when(pl.program_id(1) == 0)
    def _():
        for p in range(N_PAIRS):
            s = s0_ref[0, p]
            z = jnp.zeros_like(s)
            s_scr[p] = jnp.concatenate([jnp.where(m0, s, z), jnp.where(m0, z, s)], axis=0)
        tail_scr[...] = tail0_ref[0]

    y = _conv_silu(xbc_ref[...], convw_ref[...], tail_scr, xc_scr)

    bd = bd_ref[...]
    q = y[:, 0:WIDTH]
    k = y[:, WIDTH:2 * WIDTH]
    q_scr[...] = q * lax.rsqrt(_group_sum(q * q, bd) + RMS_EPS) * (HEAD_DIM ** -0.5)
    k_scr[...] = k * lax.rsqrt(_group_sum(k * k, bd) + RMS_EPS)
    v_scr[...] = y[:, 2 * WIDTH:3 * WIDTH]

    ab = ab_ref[...]
    z = ab + dtb_ref[...]
    softplus = jnp.maximum(z, 0.0) + jnp.log1p(jnp.exp(-jnp.abs(z)))
    g = -jnp.exp(alog_ref[...]) * softplus
    beta_scr[...] = _dot3_rhs01(_sigmoid(ab), eb_ref[...])
    gexp = _dot3_rhs01(g, eg_ref[...])
    lt = lt_ref[...]
    for c in range(nchunks):
        gc_scr[c * CHUNK:(c + 1) * CHUNK, :] = _dot3_lhs01(lt, gexp[c * CHUNK:(c + 1) * CHUNK])

    cj = jnp.bitwise_and(ci, HEAD_DIM - 1)
    incl = ri >= cj
    strict = ri > cj
    eye2 = eye2_ref[...]
    ones = jnp.ones((CHUNK, CHUNK), BF16)
    r2 = lax.broadcasted_iota(I32, (LANES, LANES), 0)
    c2 = lax.broadcasted_iota(I32, (LANES, LANES), 1)
    eye128 = jnp.where(r2 == c2, 1.0, 0.0).astype(BF16)
    same_head = (r2 < HEAD_DIM) == (c2 < HEAD_DIM)

    def tiles(body, unroll):
        def step(c, carry):
            if isinstance(c, int):
                rs = slice(c * CHUNK, (c + 1) * CHUNK)
            else:
                rs = pl.ds(pl.multiple_of(c * CHUNK, CHUNK), CHUNK)
            for p in range(N_PAIRS):
                body(rs, slice(p * LANES, (p + 1) * LANES), c * N_PAIRS + p, p)
            return carry
        if nchunks <= unroll:
            for c in range(nchunks):
                step(c, 0)
        else:
            lax.fori_loop(0, nchunks, step, 0, unroll=unroll)

    def setup(rs, cols, tile, p):
        qp = q_scr[rs, cols]
        kp = k_scr[rs, cols]
        beta = beta_scr[rs, cols]
        gc = gc_scr[rs, cols]
        gl = gc[CHUNK - 1:CHUNK, :]
        eg = jnp.exp(gc)
        kb = kp * beta
        bdk = _blockdiag(kp, m0)
        kk = _dot_nt(kb.astype(BF16), bdk)
        qk = _dot_nt(qp.astype(BF16), bdk)
        gcol = _dot3_lhs01(ones, gc * eye2)
        decay = jnp.where(incl, jnp.exp(jnp.where(incl, gc - gcol, 0.0)), 0.0)
        nmat = -(kk * jnp.where(strict, decay, 0.0))
        pw_scr[rs, cols] = nmat
        t_scr[rs, cols] = eye2 + nmat
        aqk_scr[rs, cols] = qk * decay
        q_scr[rs, cols] = qp * eg
        kd = (kp * jnp.exp(gl - gc)).astype(BF16)
        kdt_scr[tile] = _dot_nt(eye128, kd).astype(BF16)
        v_scr[rs, cols] = v_scr[rs, cols] * beta
        beta_scr[rs, cols] = kb * eg

    tiles(setup, 8)

    def double(rs, cols, tile, p):
        pw = pw_scr[rs, cols]
        pw = _pairdot(pw, pw, m0)
        pw_scr[rs, cols] = pw
        t = t_scr[rs, cols]
        t_scr[rs, cols] = t + _pairdot(t, pw, m0)

    for _ in range(5):
        tiles(double, 8)

    def solve(rs, cols, tile, p):
        t = t_scr[rs, cols]
        v_scr[rs, cols] = _pairdot(t, v_scr[rs, cols], m0)
        beta_scr[rs, cols] = _pairdot(t, beta_scr[rs, cols], m0)

    tiles(solve, 8)

    def scan_u(rs, cols, tile, p):
        u_scr[p] = v_scr[rs, cols] - _dot(beta_scr[rs, cols].astype(BF16), s_scr[p].astype(BF16))

    def scan_s(rs, cols, tile, p):
        s = s_scr[p]
        u = u_scr[p]
        o_scr[rs, cols] = (_dot(q_scr[rs, cols].astype(BF16), s.astype(BF16))
                           + _pairdot(aqk_scr[rs, cols], u, m0))
        f = _dot(kdt_scr[tile], u.astype(BF16))
        gt = jnp.exp(gc_scr[rs, cols][CHUNK - 1:CHUNK, :])
        s_scr[p] = s * gt + jnp.where(same_head, f, 0.0)

    def scan(c, carry):
        rs = pl.ds(pl.multiple_of(c * CHUNK, CHUNK), CHUNK)
        for body in (scan_u, scan_s):
            for p in range(N_PAIRS):
                body(rs, slice(p * LANES, (p + 1) * LANES), c * N_PAIRS + p, p)
        return carry

    lax.fori_loop(0, nchunks, scan, 0)

    o = o_scr[...]
    on = o * lax.rsqrt(_group_sum(o * o, bd) * (1.0 / HEAD_DIM) + RMS_EPS) * gnorm_ref[...]
    gate = gb_ref[...]
    ob_ref[...] = (on * (gate * _sigmoid(gate))).astype(ob_ref.dtype)
    for p in range(N_PAIRS):
        s = s_scr[p]
        sout_ref[0, p] = jnp.where(m0, s[0:HEAD_DIM], s[HEAD_DIM:2 * HEAD_DIM])


def _gdn(xbc, gb, ab, convw, alog, dtb, gnorm, s0, tail0, consts, *, batch, seq, rows):
    nblk = seq // rows
    bd, eb, eg, lt, eye2 = consts
    blk = lambda b, j: (b * nblk + j, 0)
    c2 = lambda b, j: (0, 0)
    per_b3 = lambda b, j: (b, 0, 0)
    per_b4 = lambda b, j: (b, 0, 0, 0)
    kern = functools.partial(_gdn_kernel, rows=rows)
    wide = pltpu.VMEM((rows, WIDTH), F32)
    return pl.pallas_call(
        kern,
        grid=(batch, nblk),
        in_specs=[
            pl.BlockSpec((rows, 3 * WIDTH), blk),
            pl.BlockSpec((rows, WIDTH), blk),
            pl.BlockSpec((rows, LANES), blk),
            pl.BlockSpec(convw.shape, c2),
            pl.BlockSpec(alog.shape, c2),
            pl.BlockSpec(dtb.shape, c2),
            pl.BlockSpec(gnorm.shape, c2),
            pl.BlockSpec((1, N_PAIRS, HEAD_DIM, LANES), per_b4),
            pl.BlockSpec((1, CONV_TAIL, 3 * WIDTH), per_b3),
            pl.BlockSpec(bd.shape, c2),
            pl.BlockSpec(eb.shape, c2),
            pl.BlockSpec(eg.shape, c2),
            pl.BlockSpec(lt.shape, c2),
            pl.BlockSpec(eye2.shape, c2),
        ],
        out_specs=[
            pl.BlockSpec((rows, WIDTH), blk),
            pl.BlockSpec((1, N_PAIRS, HEAD_DIM, LANES), per_b4),
        ],
        out_shape=[
            jax.ShapeDtypeStruct((batch * seq, WIDTH), BF16),
            jax.ShapeDtypeStruct((batch, N_PAIRS, HEAD_DIM, LANES), F32),
        ],
        scratch_shapes=[
            pltpu.VMEM((N_PAIRS, LANES, LANES), F32),
            pltpu.VMEM((CONV_TAIL, 3 * WIDTH), F32),
            pltpu.VMEM((CONV_TAIL + rows, 3 * WIDTH), F32),
            pltpu.VMEM((rows // CHUNK * N_PAIRS, LANES, HEAD_DIM), BF16),
            pltpu.VMEM((N_PAIRS, CHUNK, LANES), F32),
            wide, wide, wide, wide, wide, wide, wide, wide, wide,
        ],
        compiler_params=_params(("parallel", "arbitrary")),
        name="gdn",
    )(xbc, gb, ab, convw, alog, dtb, gnorm, s0, tail0, bd, eb, eg, lt, eye2)


def _post_kernel(oa_ref, ob_ref, gate_ref, x_ref, wa_ref, wb_ref, wo_ref, nf_ref, wr_ref,
                 x1_ref, h2_ref, route_ref):
    d = x_ref.shape[1]
    a = _dot(oa_ref[...], wa_ref[...])
    b = _dot(ob_ref[...], wb_ref[...])
    merged = _sigmoid(gate_ref[:, 0:d]) * a + _sigmoid(gate_ref[:, d:2 * d]) * b
    x1 = x_ref[...] + _dot(merged.astype(BF16), wo_ref[...])
    x1_ref[...] = x1
    h2 = x1 * lax.rsqrt(jnp.mean(x1 * x1, axis=-1, keepdims=True) + RMS_EPS) * nf_ref[...]
    h2_ref[...] = h2

    hi, lo = _split2(h2)
    wr = wr_ref[...]
    hw = _dot(hi, wr)
    logits = hw[:, 0:LANES] + (_dot(lo, wr[:, 0:LANES]) + hw[:, LANES:])

    lane = lax.broadcasted_iota(I32, logits.shape, 1).astype(F32)
    big = float(LANES)
    ninf = -jnp.inf
    lg = jnp.where(lane < N_GROUPS, logits, ninf)
    mg = jnp.max(lg, axis=-1, keepdims=True)
    grp = jnp.min(jnp.where(lg == mg, lane, big), axis=-1, keepdims=True)
    p_grp = 1.0 / jnp.sum(jnp.exp(lg - mg), axis=-1, keepdims=True)
    lo_lane = N_GROUPS + grp * EXPERTS_PER_GROUP
    le = jnp.where(jnp.logical_and(lane >= lo_lane, lane < lo_lane + EXPERTS_PER_GROUP), logits, ninf)
    m1 = jnp.max(le, axis=-1, keepdims=True)
    i1 = jnp.min(jnp.where(le == m1, lane, big), axis=-1, keepdims=True)
    le2 = jnp.where(lane == i1, ninf, le)
    m2 = jnp.max(le2, axis=-1, keepdims=True)
    i2 = jnp.min(jnp.where(le2 == m2, lane, big), axis=-1, keepdims=True)
    e2 = jnp.exp(m2 - m1)
    w1 = 1.0 / (1.0 + e2) * p_grp
    w2 = e2 / (1.0 + e2) * p_grp
    out = jnp.where(lane == 0, i1 - N_GROUPS, 0.0)
    out = jnp.where(lane == 1, i2 - N_GROUPS, out)
    out = jnp.where(lane == 2, w1, out)
    out = jnp.where(lane == 3, w2, out)
    route_ref[...] = out


def _post(oa, ob, gate, x, wa, wb, wo, nf, wr, tm):
    n, d = x.shape
    row = lambda i: (i, 0)
    const = lambda i: (0, 0)
    return pl.pallas_call(
        _post_kernel,
        grid=(n // tm,),
        in_specs=[
            pl.BlockSpec((tm, WIDTH), row), pl.BlockSpec((tm, WIDTH), row),
            pl.BlockSpec((tm, 2 * d), row), pl.BlockSpec((tm, d), row),
            pl.BlockSpec(wa.shape, const), pl.BlockSpec(wb.shape, const), pl.BlockSpec(wo.shape, const),
            pl.BlockSpec(nf.shape, const), pl.BlockSpec(wr.shape, const),
        ],
        out_specs=[pl.BlockSpec((tm, d), row), pl.BlockSpec((tm, d), row), pl.BlockSpec((tm, LANES), row)],
        out_shape=[
            jax.ShapeDtypeStruct((n, d), F32),
            jax.ShapeDtypeStruct((n, d), F32),
            jax.ShapeDtypeStruct((n, LANES), F32),
        ],
        compiler_params=_params(("parallel",)),
        name="post",
    )(oa, ob, gate, x, wa, wb, wo, nf, wr)


def _rank_kernel(route_ref, tril_ref, rank_ref, cnt_ref, carry_scr):
    @pl.when(pl.program_id(0) == 0)
    def _():
        carry_scr[...] = jnp.zeros_like(carry_scr)

    r = route_ref[...]
    lane = lax.broadcasted_iota(I32, r.shape, 1)
    lanef = lane.astype(F32)
    oh1 = lanef == r[:, 0:1]
    oh2 = lanef == r[:, 1:2]
    oh = jnp.where(jnp.logical_or(oh1, oh2), 1.0, 0.0)
    before = _dot(tril_ref[...], oh.astype(BF16)) + carry_scr[...]
    rank1 = jnp.sum(jnp.where(oh1, before, 0.0), axis=-1, keepdims=True)
    rank2 = jnp.sum(jnp.where(oh2, before, 0.0), axis=-1, keepdims=True)
    rank_ref[...] = jnp.where(lane == 0, rank1, jnp.where(lane == 1, rank2, 0.0))
    carry_scr[...] = carry_scr[...] + jnp.sum(oh, axis=0, keepdims=True)
    cnt_ref[...] = carry_scr[...]


def _rank(route, tril):
    n = route.shape[0]
    tr = min(tril.shape[0], n)
    tril = tril[:tr, :tr]
    return pl.pallas_call(
        _rank_kernel,
        grid=(n // tr,),
        in_specs=[pl.BlockSpec((tr, LANES), lambda i: (i, 0)), pl.BlockSpec(tril.shape, lambda i: (0, 0))],
        out_specs=[pl.BlockSpec((tr, LANES), lambda i: (i, 0)), pl.BlockSpec((1, LANES), lambda i: (0, 0))],
        out_shape=[jax.ShapeDtypeStruct((n, LANES), F32), jax.ShapeDtypeStruct((1, LANES), F32)],
        scratch_shapes=[pltpu.VMEM((1, LANES), F32)],
        compiler_params=_params(("arbitrary",)),
        name="rank",
    )(route, tril)


def _dest_kernel(route_ref, rank_ref, pstart_ref, dest_ref):
    r = route_ref[...]
    rk = rank_ref[...]
    ps = pstart_ref[...]
    lane = lax.broadcasted_iota(I32, r.shape, 1)
    lanef = lane.astype(F32)
    d1 = jnp.sum(jnp.where(lanef == r[:, 0:1], ps, 0.0), axis=-1, keepdims=True) + rk[:, 0:1]
    d2 = jnp.sum(jnp.where(lanef == r[:, 1:2], ps, 0.0), axis=-1, keepdims=True) + rk[:, 1:2]
    dest_ref[...] = jnp.where(lane == 0, d1, jnp.where(lane == 1, d2, 0.0)).astype(I32)


def _dest(route, rank, pstart, tr):
    n = route.shape[0]
    row = lambda i: (i, 0)
    return pl.pallas_call(
        _dest_kernel,
        grid=(n // tr,),
        in_specs=[pl.BlockSpec((tr, LANES), row), pl.BlockSpec((tr, LANES), row),
                  pl.BlockSpec((1, LANES), lambda i: (0, 0))],
        out_specs=pl.BlockSpec((tr, LANES), row),
        out_shape=jax.ShapeDtypeStruct((n, LANES), I32),
        compiler_params=_params(("parallel",)),
        name="dest",
    )(route, rank, pstart)


def _row_copy(src_ref, src_row, dst_ref, dst_row, sem):
    return pltpu.make_async_copy(src_ref.at[pl.ds(src_row, 1)], dst_ref.at[pl.ds(dst_row, 1)], sem)


def _for_rows(td, per_row):
    def group(i, carry):
        for j in range(ROW_UNROLL):
            per_row(i * ROW_UNROLL + j, j)
        return carry
    lax.fori_loop(0, td // ROW_UNROLL, group, 0)


def _dispatch_kernel(zblk_ref, nused_ref, dest_ref, h_ref, xs_ref, zero_ref, sem, zsem, *, nb):
    td = h_ref.shape[0]
    rows = zero_ref.shape[0]

    @pl.when(pl.program_id(0) == 0)
    def _():
        zero_ref[...] = jnp.zeros_like(zero_ref)

        def zero_copy(blk):
            return pltpu.make_async_copy(zero_ref, xs_ref.at[pl.ds(pl.multiple_of(blk * rows, rows), rows)], zsem)

        def for_zero_blocks(act):
            for e in range(N_EXPERTS):
                @pl.when(zblk_ref[e] >= 0)
                def _():
                    act(zero_copy(zblk_ref[e]))

            def tail(blk, carry):
                act(zero_copy(blk))
                return carry
            lax.fori_loop(nused_ref[0], nb, tail, 0)

        for_zero_blocks(lambda cp: cp.start())
        for_zero_blocks(lambda cp: cp.wait())

    def issue(t, j):
        _row_copy(h_ref, t, xs_ref, dest_ref[0, 0, 2 * t], sem).start(priority=0)
        _row_copy(h_ref, t, xs_ref, dest_ref[0, 0, 2 * t + 1], sem).start(priority=1)

    _for_rows(td, issue)
    whole = pltpu.make_async_copy(h_ref, xs_ref.at[pl.ds(0, td)], sem)
    whole.wait()
    whole.wait()


def _dispatch(zblk, nused, dest, h2, td, rows, nb):
    n, d = h2.shape
    grid_spec = pltpu.PrefetchScalarGridSpec(
        num_scalar_prefetch=2,
        grid=(n // td,),
        in_specs=[
            pl.BlockSpec((1, 1, 2 * td), lambda i, zb, nu: (i, 0, 0), memory_space=pltpu.SMEM),
            pl.BlockSpec((td, d), lambda i, zb, nu: (i, 0)),
        ],
        out_specs=pl.BlockSpec(memory_space=pl.ANY),
        scratch_shapes=[pltpu.VMEM((rows, d), F32), pltpu.SemaphoreType.DMA(()), pltpu.SemaphoreType.DMA(())],
    )
    return pl.pallas_call(
        functools.partial(_dispatch_kernel, nb=nb),
        grid_spec=grid_spec,
        out_shape=jax.ShapeDtypeStruct((nb * rows, d), F32),
        compiler_params=_params(("arbitrary",)),
        name="dispatch",
    )(zblk, nused, dest, h2)


def _ffn_kernel(blk_e_ref, nused_ref, xs_ref, wg_ref, wu_ref, wd_ref, ys_ref):
    del blk_e_ref
    i = pl.program_id(0)

    @pl.when(i < nused_ref[0])
    def _():
        xb = xs_ref[...].astype(BF16)
        g = _dot(xb, wg_ref[0].astype(BF16))
        u = _dot(xb, wu_ref[0].astype(BF16))
        hb = (g * _sigmoid(g)) * u
        ys_ref[...] = _dot(hb.astype(BF16), wd_ref[0].astype(BF16))

    @pl.when(i >= nused_ref[0])
    def _():
        ys_ref[...] = jnp.zeros_like(ys_ref)


def _ffn(blk_e, nused, xs, wg, wu, wd, rows):
    npad, d = xs.shape
    f = wg.shape[2]
    nb = npad // rows
    grid_spec = pltpu.PrefetchScalarGridSpec(
        num_scalar_prefetch=2,
        grid=(nb,),
        in_specs=[
            pl.BlockSpec((rows, d), lambda i, be, nu: (i, 0)),
            pl.BlockSpec((1, d, f), lambda i, be, nu: (be[i], 0, 0)),
            pl.BlockSpec((1, d, f), lambda i, be, nu: (be[i], 0, 0)),
            pl.BlockSpec((1, f, d), lambda i, be, nu: (be[i], 0, 0)),
        ],
        out_specs=pl.BlockSpec((rows, d), lambda i, be, nu: (i, 0)),
    )
    return pl.pallas_call(
        _ffn_kernel,
        grid_spec=grid_spec,
        out_shape=jax.ShapeDtypeStruct((npad, d), F32),
        compiler_params=_params(("arbitrary",)),
        name="ffn",
    )(blk_e, nused, xs, wg, wu, wd)


def _combine_kernel(dest_ref, dnext_ref, x1_ref, route_ref, nfin_ref, ys_ref, out_ref, g1_ref, g2_ref, sem):
    td = x1_ref.shape[0]
    i = pl.program_id(0)
    slot = i % 2
    other = 1 - slot

    def start_rows(dref, t, s):
        _row_copy(ys_ref, dref[0, 0, 2 * t], g1_ref.at[s], t, sem.at[s]).start(priority=0)
        _row_copy(ys_ref, dref[0, 0, 2 * t + 1], g2_ref.at[s], t, sem.at[s]).start(priority=1)

    def wait_tile(s):
        pltpu.make_async_copy(ys_ref.at[pl.ds(0, td)], g1_ref.at[s], sem.at[s]).wait()
        pltpu.make_async_copy(ys_ref.at[pl.ds(0, td)], g2_ref.at[s], sem.at[s]).wait()

    @pl.when(i == 0)
    def _():
        _for_rows(td, lambda t, j: start_rows(dest_ref, t, 0))

    wait_tile(slot)
    nfin = nfin_ref[...]

    def group(g, carry):
        for j in range(COMBINE_GROUP):
            start_rows(dnext_ref, g * COMBINE_GROUP + j, other)
        rs = pl.ds(pl.multiple_of(g * COMBINE_GROUP, COMBINE_GROUP), COMBINE_GROUP)
        r = route_ref[rs, :]
        y = x1_ref[rs, :] + (r[:, 2:3] * g1_ref[slot, rs, :] + r[:, 3:4] * g2_ref[slot, rs, :])
        out_ref[rs, :] = y * lax.rsqrt(jnp.mean(y * y, axis=-1, keepdims=True) + RMS_EPS) * nfin
        return carry

    lax.fori_loop(0, td // COMBINE_GROUP, group, 0)

    @pl.when(i == pl.num_programs(0) - 1)
    def _():
        wait_tile(other)


def _combine(dest, x1, route, nfin, ys, td):
    n, d = x1.shape
    nt = n // td
    return pl.pallas_call(
        _combine_kernel,
        grid=(nt,),
        in_specs=[
            pl.BlockSpec((1, 1, 2 * td), lambda i: (i, 0, 0), memory_space=pltpu.SMEM),
            pl.BlockSpec((1, 1, 2 * td), lambda i: (jnp.minimum(i + 1, nt - 1), 0, 0), memory_space=pltpu.SMEM),
            pl.BlockSpec((td, d), lambda i: (i, 0)),
            pl.BlockSpec((td, LANES), lambda i: (i, 0)),
            pl.BlockSpec((1, d), lambda i: (0, 0)),
            pl.BlockSpec(memory_space=pl.ANY),
        ],
        out_specs=pl.BlockSpec((td, d), lambda i: (i, 0)),
        out_shape=jax.ShapeDtypeStruct((n, d), F32),
        scratch_shapes=[pltpu.VMEM((2, td, d), F32), pltpu.VMEM((2, td, d), F32), pltpu.SemaphoreType.DMA((2,))],
        compiler_params=_params(("arbitrary",)),
        name="combine",
    )(dest, dest, x1, route, nfin, ys)


def _moe(x1, h2, route, wg, wu, wd, nfin, tril, td):
    n, d = x1.shape
    rank, cnt = _rank(route, tril)
    rows = MOE_ROWS if 2 * n >= N_EXPERTS * MOE_ROWS else MOE_ROWS_SMALL
    counts = cnt[0, :N_EXPERTS].astype(I32)
    pcounts = (counts + rows - 1) // rows * rows
    pends = jnp.cumsum(pcounts)
    pstarts = pends - pcounts
    nb = (2 * n) // rows + N_EXPERTS
    blk_e = jnp.minimum(
        jnp.sum(pends[None, :] <= (jnp.arange(nb, dtype=I32) * rows)[:, None], axis=1), N_EXPERTS - 1
    ).astype(I32)
    nused = (pends[-1:] // rows).astype(I32)
    pstart_row = jnp.pad(pstarts.astype(F32), (0, LANES - N_EXPERTS))[None, :]
    dest = _dest(route, rank, pstart_row, min(n, 1024))[:, 0:2].reshape(n // td, 1, 2 * td)
    zblk = jnp.where(pcounts > 0, pends // rows - 1, -1).astype(I32)
    xs = _dispatch(zblk, nused, dest, h2, td, rows, nb)
    ys = _ffn(blk_e, nused, xs, wg, wu, wd, rows)
    return _combine(dest, x1, route, nfin, ys, td)


def _layer(x, past_k, past_v, s0, conv_buf, wts, consts, *, prompt):
    (norm_mix, w_in_parts, bias, conv_w, alog, dtb, gnorm, wa, wb, wo, norm_ffn, wr,
     wg, wu, wd, norm_final) = wts
    batch, seq, d = x.shape
    n = batch * seq
    left = LEFT_CHUNKS * CHUNK
    tm = 512 if n % 512 == 0 else n
    xf = x.reshape(n, d)
    keep = min(left, seq)
    assert (seq % tm == 0 and tm == keep) or (tm % seq == 0 and keep == seq)
    tiles_per_seq = max(seq // tm, 1)
    qkv, kv, xbc, gb, ab, gate = _inproj(xf, norm_mix, w_in_parts, tm, tiles_per_seq)

    if prompt:
        o_a = _attention(qkv, qkv, qkv, bias, batch=batch, seq=seq, cpb=LEFT_CHUNKS, prev_is_cache=False)
        rows = left
        s_init = jnp.zeros((batch, N_PAIRS, HEAD_DIM, LANES), F32)
        tail = jnp.zeros((batch, CONV_TAIL, 3 * WIDTH), F32)
    else:
        cache = jnp.concatenate(
            [past_k.reshape(batch * left, WIDTH), past_v.reshape(batch * left, WIDTH)], axis=1).astype(BF16)
        o_a = _attention(qkv, cache, qkv, bias, batch=batch, seq=seq, cpb=seq // CHUNK, prev_is_cache=True)
        rows = seq
        s_init = s0.reshape(batch, N_PAIRS, 2, HEAD_DIM, HEAD_DIM).transpose(0, 1, 3, 2, 4).reshape(
            batch, N_PAIRS, HEAD_DIM, LANES)
        tail = jnp.pad(conv_buf, ((0, 0), (CONV_TAIL - (CONV_W - 1), 0), (0, 0)))

    bd, eb, eg, lt, eye2, tril = consts
    o_b, s_new = _gdn(xbc, gb, ab, conv_w, alog, dtb, gnorm, s_init, tail,
                      (bd, eb, eg, lt, eye2), batch=batch, seq=seq, rows=rows)

    x1, h2, route = _post(o_a, o_b, gate, xf, wa, wb, wo, norm_ffn, wr, tm)
    y = _moe(x1, h2, route, wg, wu, wd, norm_final, tril, td=min(1024, n))

    kv4 = kv.reshape(batch, keep, 2, N_HEADS, HEAD_DIM)
    new_k = kv4[:, :, 0]
    new_v = kv4[:, :, 1]
    new_s = s_new.reshape(batch, N_PAIRS, HEAD_DIM, 2, HEAD_DIM).transpose(0, 1, 3, 2, 4).reshape(
        batch, N_HEADS, HEAD_DIM, HEAD_DIM)
    new_conv = xbc.reshape(batch, seq, 3 * WIDTH)[:, seq - (CONV_W - 1):]
    return y.reshape(batch, seq, d), new_k, new_v, new_s, new_conv


def _constants():
    lane = jnp.arange(WIDTH)
    src = jnp.arange(LANES)
    bd = (src[:, None] // HEAD_DIM == src[None, :] // HEAD_DIM).astype(BF16)
    eb = (src[:, None] == lane[None, :] // HEAD_DIM).astype(BF16)
    eg = (src[:, None] == N_HEADS + lane[None, :] // HEAD_DIM).astype(BF16)
    c = jnp.arange(CHUNK)
    lt = (c[:, None] >= c[None, :]).astype(BF16)
    eye2 = (c[:, None] == jnp.arange(LANES)[None, :] % HEAD_DIM).astype(F32)
    t = jnp.arange(512)
    tril = (t[:, None] > t[None, :]).astype(BF16)
    return bd, eb, eg, lt, eye2, tril


def kernel(x_prompt, x_sample, cache_attn_k, cache_attn_v, state_delta, state_conv, norm_mix, w_in, rel_bias,
           conv_w, a_log, dt_bias, gdn_norm, w_branch_a, w_branch_b, w_out, norm_ffn, w_route_group,
           w_route_expert, w_gate, w_up, w_down, norm_final):
    depth = w_in.shape[0]
    d = x_prompt.shape[-1]
    left = LEFT_CHUNKS * CHUNK
    consts = _constants()

    def bias_table(rb):
        nd = BAND + CHUNK - 1
        d_rev = (left + CHUNK - 1) - jnp.arange(nd)
        vr = rb[:, jnp.clip(d_rev, -REL_CLIP, REL_CLIP) + REL_CLIP]
        return jnp.stack([vr[:, CHUNK - 1 - i:CHUNK - 1 - i + BAND] for i in range(CHUNK)], axis=1)

    xp, xs = x_prompt, x_sample
    outs_p, outs_s = [], []
    for l in range(depth):
        w = w_in[l]
        o = 0
        parts = []
        for width in (3 * WIDTH, 3 * WIDTH, WIDTH, 2 * N_HEADS, 2 * d):
            parts.append(w[:, o:o + width])
            o += width
        wqkv, wxbc, wgb, wab, wgate = parts
        wqkv = jnp.concatenate([wqkv[:, :WIDTH] * (HEAD_DIM ** -0.5), wqkv[:, WIDTH:]], axis=1)
        wab = jnp.pad(wab, ((0, 0), (0, LANES - 2 * N_HEADS)))
        w_in_parts = tuple(m.astype(BF16) for m in (wqkv, wxbc, wgb, wab, wgate))
        lane_pad = lambda v: jnp.pad(v[None, :], ((0, 0), (N_HEADS, LANES - 2 * N_HEADS)))
        wr = jnp.pad(jnp.concatenate([w_route_group[l], w_route_expert[l]], axis=1),
                     ((0, 0), (0, LANES - N_GROUPS - N_EXPERTS)))
        wrh = wr.astype(BF16)
        wrl = (wr - wrh.astype(F32)).astype(BF16)
        wts = (
            norm_mix[l][None, :], w_in_parts, bias_table(rel_bias[l]), conv_w[l], lane_pad(a_log[l]),
            lane_pad(dt_bias[l]), jnp.tile(gdn_norm[l], N_HEADS)[None, :],
            w_branch_a[l].astype(BF16), w_branch_b[l].astype(BF16), w_out[l].astype(BF16),
            norm_ffn[l][None, :], jnp.concatenate([wrh, wrl], axis=1),
            w_gate[l], w_up[l], w_down[l], norm_final[None, :],
        )
        assert depth == 1
        xp, kp, vp, sp, cp = _layer(xp, None, None, None, None, wts, consts, prompt=True)
        xs, ks, vs, ss, cs = _layer(xs, cache_attn_k[l], cache_attn_v[l], state_delta[l], state_conv[l],
                                    wts, consts, prompt=False)
        outs_p.append((kp, vp, sp, cp))
        outs_s.append((ks, vs, ss, cs))
    stack = lambda items, idx: jnp.stack([it[idx] for it in items])
    return (xp, xs,
            stack(outs_p, 0), stack(outs_p, 1), stack(outs_p, 2), stack(outs_p, 3),
            stack(outs_s, 0), stack(outs_s, 1), stack(outs_s, 2), stack(outs_s, 3))
```
